```python
import jax, jax.numpy as jnp
from jax import lax
import numpy as np

D_MODEL = 2048
BATCH = 2
SEQ = 4096
DEPTH = 4
DEC_BATCH = 32
DEC_SEQ = 32
PAST_LEN = 1024

CHUNK = 64
CONV_WIDTH = D_MODEL // 2
CONV_K = 31
POOL_WIDTH = D_MODEL // 2
POOL_WINDOWS = (2, 4, 8, 16)
POOL_GROUPS = 4
POOL_GROUP = POOL_WIDTH // POOL_GROUPS
POOL_HIST = 15
GMLP_WIDTH = 2 * D_MODEL
GMLP_HEADS = 4
GMLP_HEAD_DIM = GMLP_WIDTH // GMLP_HEADS
GMLP_BLOCK = 128
D_FF = 5632
N_EXPERTS = 8
TOP_K = 2
D_FF_EXPERT = 7168
MOE_BLOCK = 128
N_EVEN = (DEPTH + 1) // 2
N_ODD = DEPTH // 2
RMS_EPS = 1e-6
LN_EPS = 1e-5

kernel_name = 'chunk_causal_conv_pool_gmlp_moe_encoder_step'


def _rmsnorm(x, g):
    xf = x.astype(jnp.float32)
    y = xf * lax.rsqrt(jnp.mean(xf * xf, axis=-1, keepdims=True) + RMS_EPS)
    return (y * g.astype(jnp.float32)).astype(x.dtype)


def _layernorm(x, g, b):
    xf = x.astype(jnp.float32)
    mu = jnp.mean(xf, axis=-1, keepdims=True)
    var = jnp.mean(jnp.square(xf - mu), axis=-1, keepdims=True)
    y = (xf - mu) * lax.rsqrt(var + LN_EPS) * g.astype(jnp.float32) + b.astype(jnp.float32)
    return y.astype(x.dtype)


def _swiglu(x, w_gate, w_up, w_down):
    return (jax.nn.silu(x @ w_gate) * (x @ w_up)) @ w_down


def _causal_depthwise(c_ext, w, b):
    n_ch = c_ext.shape[-1]
    y = lax.conv_general_dilated(
        c_ext, w[:, None, :].astype(c_ext.dtype), window_strides=(1,), padding='VALID',
        dimension_numbers=('NWC', 'WIO', 'NWC'), feature_group_count=n_ch)
    return y + b.astype(c_ext.dtype)


def _multiscale_pool(p_ext, pos0, w_grp, scale):
    n_b, n_ext, _ = p_ext.shape
    n_new = n_ext - POOL_HIST
    pf = p_ext.astype(jnp.float32)
    cs = jnp.pad(jnp.cumsum(pf, axis=1), ((0, 0), (1, 0), (0, 0)))
    pos = pos0 + jnp.arange(n_new)
    hi = POOL_HIST + 1
    cur = pf[:, POOL_HIST:]
    groups = []
    for gi, w in enumerate(POOL_WINDOWS):
        sl = slice(gi * POOL_GROUP, (gi + 1) * POOL_GROUP)
        win_sum = cs[:, hi:hi + n_new, sl] - cs[:, hi - w:hi - w + n_new, sl]
        cnt = jnp.minimum(pos + 1, w).astype(jnp.float32)[None, :, None]
        groups.append(win_sum / cnt - cur[..., sl])
    pooled = jnp.stack(groups, axis=2).astype(p_ext.dtype)
    mixed = jnp.einsum('btgc,gcd->btgd', pooled, w_grp)
    return mixed.reshape(n_b, n_new, POOL_WIDTH) * scale


def _ab_mixer(h, conv_hist, pool_hist, pos0, w_in, conv_w, conv_b, ln_g, ln_b,
              pool_w, pool_scale, w_out):
    z = h @ w_in
    a = z[..., :CONV_WIDTH]
    gt = z[..., CONV_WIDTH:2 * CONV_WIDTH]
    xp = z[..., 2 * CONV_WIDTH:]
    c = a * jax.nn.sigmoid(gt)
    c_ext = jnp.concatenate([conv_hist.astype(c.dtype), c], axis=1)
    ca = jax.nn.silu(_layernorm(_causal_depthwise(c_ext, conv_w, conv_b), ln_g, ln_b))
    p_ext = jnp.concatenate([pool_hist.astype(xp.dtype), xp], axis=1)
    pb = _multiscale_pool(p_ext, pos0, pool_w, pool_scale)
    out = jnp.concatenate([ca, pb.astype(ca.dtype)], axis=-1) @ w_out
    return out, c_ext[:, -(CONV_K - 1):], p_ext[:, -POOL_HIST:]


def _gmlp_mixer(h, w_in, b_in, ln_g, ln_b, w_s, b_s, w_out):
    n_b, n_t, _ = h.shape
    z = jax.nn.gelu(h @ w_in + b_in, approximate=False)
    u, v = z[..., :GMLP_WIDTH], z[..., GMLP_WIDTH:]
    vn = _layernorm(v, ln_g, ln_b)
    blk = GMLP_BLOCK if n_t % GMLP_BLOCK == 0 else n_t
    n_blk = n_t // blk
    idx = jnp.arange(blk)
    mask = (idx[None, :] // CHUNK) <= (idx[:, None] // CHUNK)
    ws = jnp.where(mask[None], w_s[:, :blk, :blk], 0.0)
    vb = vn.reshape(n_b, n_blk, blk, GMLP_HEADS, GMLP_HEAD_DIM)
    s = jnp.einsum('hij,bnjhc->bnihc', ws, vb) + jnp.transpose(b_s[:, :blk])[None, None, :, :, None]
    gated = u.reshape(n_b, n_blk, blk, GMLP_HEADS, GMLP_HEAD_DIM) * s
    return gated.reshape(n_b, n_t, GMLP_WIDTH) @ w_out, vn


def _moe_swiglu(x, w_router, w_gate, w_up, w_down):
    n_b, n_t, d = x.shape
    n_tok = n_b * n_t
    n_asg = n_tok * TOP_K
    xt = x.reshape(n_tok, d)
    logits = (xt @ w_router).astype(jnp.float32)
    top_logit, top_e = lax.top_k(logits, TOP_K)
    gate = jax.nn.softmax(top_logit, axis=-1)
    flat_e = top_e.reshape(-1)
    flat_tok = jnp.repeat(jnp.arange(n_tok), TOP_K)
    flat_g = gate.reshape(-1)
    order = jnp.argsort(flat_e)
    se, stok, sg = flat_e[order], flat_tok[order], flat_g[order]
    counts = jnp.bincount(flat_e, length=N_EXPERTS)
    starts = jnp.cumsum(counts) - counts
    padded = (counts + MOE_BLOCK - 1) // MOE_BLOCK * MOE_BLOCK
    pends = jnp.cumsum(padded)
    pstarts = pends - padded
    dest = pstarts[se] + jnp.arange(n_asg) - starts[se]
    n_blocks = (n_asg + MOE_BLOCK - 1) // MOE_BLOCK + N_EXPERTS
    rows = jnp.zeros((n_blocks * MOE_BLOCK, d), x.dtype).at[dest].set(xt[stok])
    block_e = jnp.minimum(jnp.searchsorted(pends, jnp.arange(n_blocks) * MOE_BLOCK, side='right'),
                          N_EXPERTS - 1)

    def expert_block(args):
        xb, e = args
        return _swiglu(xb, w_gate[e], w_up[e], w_down[e])

    out_rows = lax.map(expert_block, (rows.reshape(n_blocks, MOE_BLOCK, d), block_e))
    contrib = out_rows.reshape(-1, d)[dest] * sg[:, None].astype(x.dtype)
    y = jnp.zeros_like(xt).at[stok].add(contrib)
    return y.reshape(n_b, n_t, d)


def setup_inputs(seed: int = 0) -> dict:
    key = jax.random.key(seed)
    ks = iter(jax.random.split(key, 40))
    f32 = jnp.float32

    def normal(shape, scale):
        return jax.random.normal(next(ks), shape, f32) * scale

    def gain(shape):
        return 1.0 + normal(shape, 0.05)

    E, O = N_EVEN, N_ODD
    return {
        'x_prompt': normal((BATCH, SEQ, D_MODEL), 1.0),
        'x_sample': normal((DEC_BATCH, DEC_SEQ, D_MODEL), 1.0),
        'state_conv': normal((E, DEC_BATCH, CONV_K - 1, CONV_WIDTH), 0.5),
        'state_pool': normal((E, DEC_BATCH, POOL_HIST, POOL_WIDTH), 1.0),
        'even_mix_norm': gain((E, D_MODEL)),
        'even_w_in': normal((E, D_MODEL, 2 * CONV_WIDTH + POOL_WIDTH), D_MODEL ** -0.5),
        'even_conv_w': normal((E, CONV_K, CONV_WIDTH), CONV_K ** -0.5),
        'even_conv_b': normal((E, CONV_WIDTH), 0.02),
        'even_conv_ln_g': gain((E, CONV_WIDTH)),
        'even_conv_ln_b': normal((E, CONV_WIDTH), 0.02),
        'even_pool_w': normal((E, POOL_GROUPS, POOL_GROUP, POOL_GROUP), POOL_GROUP ** -0.5),
        'even_pool_scale': 1.0 + normal((E, POOL_WIDTH), 0.1),
        'even_w_out': normal((E, CONV_WIDTH + POOL_WIDTH, D_MODEL), (CONV_WIDTH + POOL_WIDTH) ** -0.5),
        'even_ffn_norm': gain((E, D_MODEL)),
        'even_ffn_w_gate': normal((E, D_MODEL, D_FF), D_MODEL ** -0.5),
        'even_ffn_w_up': normal((E, D_MODEL, D_FF), D_MODEL ** -0.5),
        'even_ffn_w_down': normal((E, D_FF, D_MODEL), D_FF ** -0.5),
        'odd_mix_norm': gain((O, D_MODEL)),
        'odd_w_in': normal((O, D_MODEL, 2 * GMLP_WIDTH), D_MODEL ** -0.5),
        'odd_b_in': normal((O, 2 * GMLP_WIDTH), 0.02),
        'odd_v_ln_g': gain((O, GMLP_WIDTH)),
        'odd_v_ln_b': normal((O, GMLP_WIDTH), 0.02),
        'odd_w_s': normal((O, GMLP_HEADS, GMLP_BLOCK, GMLP_BLOCK), 0.5 * GMLP_BLOCK ** -0.5),
        'odd_b_s': 1.0 + normal((O, GMLP_HEADS, GMLP_BLOCK), 0.1),
        'odd_w_out': normal((O, GMLP_WIDTH, D_MODEL), GMLP_WIDTH ** -0.5),
        'odd_ffn_norm': gain((O, D_MODEL)),
        'odd_router_w': normal((O, D_MODEL, N_EXPERTS), D_MODEL ** -0.5),
        'odd_moe_w_gate': normal((O, N_EXPERTS, D_MODEL, D_FF_EXPERT), D_MODEL ** -0.5),
        'odd_moe_w_up': normal((O, N_EXPERTS, D_MODEL, D_FF_EXPERT), D_MODEL ** -0.5),
        'odd_moe_w_down': normal((O, N_EXPERTS, D_FF_EXPERT, D_MODEL), D_FF_EXPERT ** -0.5),
        'final_norm': gain((D_MODEL,)),
    }


def reference(x_prompt, x_sample, state_conv, state_pool,
              even_mix_norm, even_w_in, even_conv_w, even_conv_b, even_conv_ln_g, even_conv_ln_b,
              even_pool_w, even_pool_scale, even_w_out,
              even_ffn_norm, even_ffn_w_gate, even_ffn_w_up, even_ffn_w_down,
              odd_mix_norm, odd_w_in, odd_b_in, odd_v_ln_g, odd_v_ln_b, odd_w_s, odd_b_s, odd_w_out,
              odd_ffn_norm, odd_router_w, odd_moe_w_gate, odd_moe_w_up, odd_moe_w_down,
              final_norm):
    xp, xs = x_prompt, x_sample
    conv_p, pool_p, conv_s, pool_s, v_s = [], [], [], [], []
    for layer in range(DEPTH):
        li = layer // 2
        if layer % 2 == 0:
            ab = (even_w_in[li], even_conv_w[li], even_conv_b[li], even_conv_ln_g[li],
                  even_conv_ln_b[li], even_pool_w[li], even_pool_scale[li], even_w_out[li])
            zc = jnp.zeros((xp.shape[0], CONV_K - 1, CONV_WIDTH), xp.dtype)
            zp = jnp.zeros((xp.shape[0], POOL_HIST, POOL_WIDTH), xp.dtype)
            mp, cp, pp = _ab_mixer(_rmsnorm(xp, even_mix_norm[li]), zc, zp, 0, *ab)
            ms, cs, ps = _ab_mixer(_rmsnorm(xs, even_mix_norm[li]), state_conv[li], state_pool[li],
                                   PAST_LEN, *ab)
            xp = xp + mp
            xs = xs + ms
            conv_p.append(cp)
            pool_p.append(pp)
            conv_s.append(cs)
            pool_s.append(ps)
            ffn = (even_ffn_w_gate[li], even_ffn_w_up[li], even_ffn_w_down[li])
            xp = xp + _swiglu(_rmsnorm(xp, even_ffn_norm[li]), *ffn)
            xs = xs + _swiglu(_rmsnorm(xs, even_ffn_norm[li]), *ffn)
        else:
            gm = (odd_w_in[li], odd_b_in[li], odd_v_ln_g[li], odd_v_ln_b[li],
                  odd_w_s[li], odd_b_s[li], odd_w_out[li])
            mp, _ = _gmlp_mixer(_rmsnorm(xp, odd_mix_norm[li]), *gm)
            ms, vs = _gmlp_mixer(_rmsnorm(xs, odd_mix_norm[li]), *gm)
            xp = xp + mp
            xs = xs + ms
            v_s.append(vs)
            moe = (odd_router_w[li], odd_moe_w_gate[li], odd_moe_w_up[li], odd_moe_w_down[li])
            xp = xp + _moe_swiglu(_rmsnorm(xp, odd_ffn_norm[li]), *moe)
            xs = xs + _moe_swiglu(_rmsnorm(xs, odd_ffn_norm[li]), *moe)
    y_prompt = _rmsnorm(xp, final_norm)
    y_sample = _rmsnorm(xs, final_norm)
    new_conv_prompt = jnp.stack(conv_p)
    new_pool_prompt = jnp.stack(pool_p)
    new_conv_sample = jnp.stack(conv_s)
    new_pool_sample = jnp.stack(pool_s)
    new_gmlp_v_sample = jnp.stack(v_s)
    return (y_prompt, y_sample, new_conv_prompt, new_pool_prompt, new_conv_sample, new_pool_sample, new_gmlp_v_sample)
```

```python
import functools
import math

import jax
import jax.numpy as jnp
from jax import lax
from jax.experimental import pallas as pl
from jax.experimental.pallas import tpu as pltpu

F32 = jnp.float32
BF16 = jnp.bfloat16

CHUNK = 64
GMLP_BLOCK = 128
POOL_WINDOWS = (2, 4, 8, 16)
PAST_LEN = 1024
TOP_K = 2
RMS_EPS = 1e-6
LN_EPS = 1e-5

HALO = 32
V7X_VMEM_BYTES = 64 * 1024 * 1024
VMEM_LIMIT = 60000 * 1024


def _pick(n, target, mult):
    best = None
    for t in range(mult, min(n, target) + 1, mult):
        if n % t == 0:
            best = t
    assert best is not None, (n, target, mult)
    return best


def _params(*sem):
    return pltpu.CompilerParams(dimension_semantics=sem, vmem_limit_bytes=VMEM_LIMIT)


def _rms(x, g):
    return x * lax.rsqrt(jnp.mean(x * x, axis=-1, keepdims=True) + RMS_EPS) * g


def _sigmoid(x):
    return 1.0 / (1.0 + jnp.exp(-x))


def _rmsnorm_kernel(x_ref, g_ref, o_ref):
    o_ref[...] = _rms(x_ref[...], g_ref[...]).astype(o_ref.dtype)


def rmsnorm_cast(x, g, out_dtype):
    m, d = x.shape
    tb = _pick(m, 512, 16)
    return pl.pallas_call(
        _rmsnorm_kernel,
        grid=(m // tb,),
        in_specs=[pl.BlockSpec((tb, d), lambda i: (i, 0)),
                  pl.BlockSpec((1, d), lambda i: (0, 0))],
        out_specs=pl.BlockSpec((tb, d), lambda i: (i, 0)),
        out_shape=jax.ShapeDtypeStruct((m, d), out_dtype),
        compiler_params=_params("parallel"),
        name="rmsnorm_cast",
    )(x, g.reshape(1, d))


def _matmul_act_kernel(*refs, has_bias, gelu):
    if has_bias:
        a_ref, w_ref, b_ref, o_ref = refs
    else:
        a_ref, w_ref, o_ref = refs
    acc = jnp.dot(a_ref[...], w_ref[...].astype(BF16), preferred_element_type=F32)
    if has_bias:
        acc = acc + b_ref[...]
    if gelu:
        acc = 0.5 * acc * (1.0 + lax.erf(acc * (1.0 / math.sqrt(2.0))))
    o_ref[...] = acc


def matmul_act(a, w, bias=None, gelu=False):
    m, k = a.shape
    n = w.shape[1]
    tm = _pick(m, 1024, 16)
    tn = _pick(n, 512, 128)
    in_specs = [pl.BlockSpec((tm, k), lambda i, j: (i, 0)),
                pl.BlockSpec((k, tn), lambda i, j: (0, j))]
    args = [a, w]
    if bias is not None:
        in_specs.append(pl.BlockSpec((1, tn), lambda i, j: (0, j)))
        args.append(bias.reshape(1, n))
    return pl.pallas_call(
        functools.partial(_matmul_act_kernel, has_bias=bias is not None, gelu=gelu),
        grid=(m // tm, n // tn),
        in_specs=in_specs,
        out_specs=pl.BlockSpec((tm, tn), lambda i, j: (i, j)),
        out_shape=jax.ShapeDtypeStruct((m, n), F32),
        compiler_params=_params("parallel", "parallel"),
        name="matmul_act",
    )(*args)


def _matmul_resid_kernel(*refs, has_norm):
    if has_norm:
        a_ref, w_ref, r_ref, g_ref, x_ref, h_ref = refs
    else:
        a_ref, w_ref, r_ref, x_ref = refs
    k = pl.program_id(1)
    part = jnp.dot(a_ref[...], w_ref[...].astype(BF16), preferred_element_type=F32)

    @pl.when(k == 0)
    def _():
        x_ref[...] = r_ref[...] + part

    @pl.when(k > 0)
    def _():
        x_ref[...] += part

    if has_norm:
        @pl.when(k == pl.num_programs(1) - 1)
        def _():
            h_ref[...] = _rms(x_ref[...], g_ref[...]).astype(h_ref.dtype)


def matmul_resid(a, w, resid, norm_g=None):
    m, k = a.shape
    d = w.shape[1]
    tm = _pick(m, 512, 16)
    tk = _pick(k, 512, 128)
    in_specs = [pl.BlockSpec((tm, tk), lambda i, kk: (i, kk)),
                pl.BlockSpec((tk, d), lambda i, kk: (kk, 0)),
                pl.BlockSpec((tm, d), lambda i, kk: (i, 0))]
    args = [a, w, resid]
    out_specs = [pl.BlockSpec((tm, d), lambda i, kk: (i, 0))]
    out_shape = [jax.ShapeDtypeStruct((m, d), F32)]
    if norm_g is not None:
        in_specs.append(pl.BlockSpec((1, d), lambda i, kk: (0, 0)))
        args.append(norm_g.reshape(1, d))
        out_specs.append(pl.BlockSpec((tm, d), lambda i, kk: (i, 0)))
        out_shape.append(jax.ShapeDtypeStruct((m, d), BF16))
    out = pl.pallas_call(
        functools.partial(_matmul_resid_kernel, has_norm=norm_g is not None),
        grid=(m // tm, k // tk),
        in_specs=in_specs,
        out_specs=out_specs,
        out_shape=out_shape,
        compiler_params=_params("parallel", "arbitrary"),
        name="matmul_resid",
    )(*args)
    return (out[0], out[1]) if norm_g is not None else (out[0], None)


def _swiglu_kernel(be_ref, bv_ref, bx_ref, *refs, ts, has_resid):
    if has_resid:
        x_ref, wg_ref, wu_ref, wd_ref, r_ref, g_ref, o_ref, h_ref, wg_s, wu_s, wd_s = refs
    else:
        x_ref, wg_ref, wu_ref, wd_ref, o_ref, wg_s, wu_s, wd_s = refs
    b = pl.program_id(0)
    f = pl.program_id(1)
    nf = pl.num_programs(1)
    nv = bv_ref[b]
    tm = x_ref.shape[0]

    @pl.when(nv > 0)
    def _():
        wg_s[...] = wg_ref[0].astype(BF16)
        wu_s[...] = wu_ref[0].astype(BF16)
        wd_s[...] = wd_ref[0].astype(BF16)
        for s in range(tm // ts):
            rows = slice(s * ts, (s + 1) * ts)

            @pl.when(s * ts < nv)
            def _():
                xs = x_ref[rows, :]
                hg = jnp.dot(xs, wg_s[...], preferred_element_type=F32)
                hu = jnp.dot(xs, wu_s[...], preferred_element_type=F32)
                hh = (hg * _sigmoid(hg) * hu).astype(BF16)
                part = jnp.dot(hh, wd_s[...], preferred_element_type=F32)

                @pl.when(f == 0)
                def _():
                    if has_resid:
                        o_ref[rows, :] = r_ref[rows, :] + part
                    else:
                        o_ref[rows, :] = part

                @pl.when(f > 0)
                def _():
                    o_ref[rows, :] += part

            @pl.when(jnp.logical_and(s * ts >= nv, f == 0))
            def _():
                o_ref[rows, :] = jnp.zeros((ts, o_ref.shape[1]), F32)

        if has_resid:
            @pl.when(f == nf - 1)
            def _():
                h_ref[...] = _rms(o_ref[...], g_ref[...]).astype(h_ref.dtype)

    @pl.when(jnp.logical_and(nv == 0, f == 0))
    def _():
        o_ref[...] = jnp.zeros(o_ref.shape, F32)


def swiglu_grouped(x, w_gate, w_up, w_down, block_e, block_valid, block_x, resid=None, norm_g=None):
    m, d = x.shape
    n_e, _, dff = w_gate.shape
    nb = block_e.shape[0]
    assert m % nb == 0
    tm = m // nb
    tf = _pick(dff, 256 if resid is not None else 512, 128)
    ts = _pick(tm, 256, 16)
    nf = dff // tf
    has_resid = resid is not None

    def x_map(b, f, be, bv, bx):
        return (bx[b], 0)

    def wcol_map(b, f, be, bv, bx):
        return (be[b], 0, jnp.where(bv[b] > 0, f, nf - 1))

    def wrow_map(b, f, be, bv, bx):
        return (be[b], jnp.where(bv[b] > 0, f, nf - 1), 0)

    once = pl.Buffered(1)
    in_specs = [pl.BlockSpec((tm, d), x_map, pipeline_mode=once),
                pl.BlockSpec((1, d, tf), wcol_map),
                pl.BlockSpec((1, d, tf), wcol_map),
                pl.BlockSpec((1, tf, d), wrow_map)]
    args = [x, w_gate, w_up, w_down]
    out_specs = [pl.BlockSpec((tm, d), lambda b, f, be, bv, bx: (b, 0))]
    out_shape = [jax.ShapeDtypeStruct((m, d), F32)]
    if has_resid:
        in_specs += [pl.BlockSpec((tm, d), x_map, pipeline_mode=once),
                     pl.BlockSpec((1, d), lambda b, f, be, bv, bx: (0, 0))]
        args += [resid, norm_g.reshape(1, d)]
        out_specs.append(pl.BlockSpec((tm, d), x_map))
        out_shape.append(jax.ShapeDtypeStruct((m, d), BF16))
    out = pl.pallas_call(
        functools.partial(_swiglu_kernel, ts=ts, has_resid=has_resid),
        grid_spec=pltpu.PrefetchScalarGridSpec(
            num_scalar_prefetch=3,
            grid=(nb, nf),
            in_specs=in_specs,
            out_specs=out_specs,
            scratch_shapes=[pltpu.VMEM((d, tf), BF16), pltpu.VMEM((d, tf), BF16),
                            pltpu.VMEM((tf, d), BF16)],
        ),
        out_shape=out_shape,
        compiler_params=_params("arbitrary", "arbitrary"),
        name="swiglu_resid" if has_resid else "swiglu_experts",
    )(block_e, block_valid, block_x, *args)
    return (out[0], out[1]) if has_resid else out[0]


def _mixer_mid_kernel(a_ref, g_ref, p_ref, ha_ref, hg_ref, hp_ref, cw_ref, cb_ref, lng_ref, lnb_ref,
                      pw_ref, ps_ref, mid_ref, clast_ref, cx_s, px_s, y_s, pool_s,
                      *, halo_glu, zero_first, pos0):
    t = pl.program_id(1)
    tt, cw = a_ref.shape
    pw = p_ref.shape[1]
    ck = cw_ref.shape[0]
    off = HALO - (ck - 1)

    c = a_ref[...] * _sigmoid(g_ref[...])
    ch = ha_ref[...]
    if halo_glu:
        ch = ch * _sigmoid(hg_ref[...])
    ph = hp_ref[...]
    if zero_first:
        ch = jnp.where(t > 0, ch, 0.0)
        ph = jnp.where(t > 0, ph, 0.0)
    cx_s[0:HALO, :] = ch
    cx_s[HALO:HALO + tt, :] = c
    px_s[0:HALO, :] = ph
    px_s[HALO:HALO + tt, :] = p_ref[...]

    @pl.when(t == pl.num_programs(1) - 1)
    def _():
        clast_ref[0] = c[tt - HALO:tt, :]

    rc = min(tt, 32)
    lc = min(cw, 512)
    for r0 in range(0, tt, rc):
        for l0 in range(0, cw, lc):
            acc = jnp.broadcast_to(cb_ref[:, l0:l0 + lc], (rc, lc))
            for k in range(ck):
                acc = acc + cw_ref[k:k + 1, l0:l0 + lc] * cx_s[r0 + off + k:r0 + off + k + rc, l0:l0 + lc]
            y_s[r0:r0 + rc, l0:l0 + lc] = acc

    y = y_s[...]
    mu = jnp.mean(y, axis=-1, keepdims=True)
    yc = y - mu
    var = jnp.mean(yc * yc, axis=-1, keepdims=True)
    yn = yc * lax.rsqrt(var + LN_EPS) * lng_ref[...] + lnb_ref[...]
    mid_ref[:, 0:cw] = (yn * _sigmoid(yn)).astype(mid_ref.dtype)

    n_grp = len(POOL_WINDOWS)
    gw = pw // n_grp
    for gi, win in enumerate(POOL_WINDOWS):
        lanes = slice(gi * gw, (gi + 1) * gw)
        for r0 in range(0, tt, rc):
            cur = px_s[HALO + r0:HALO + r0 + rc, lanes]
            ws = cur
            for i in range(1, win):
                ws = ws + px_s[HALO + r0 - i:HALO + r0 - i + rc, lanes]
            pos = pos0 + t * tt + r0 + lax.broadcasted_iota(jnp.int32, (rc, 1), 0)
            cnt = jnp.minimum(pos + 1, win).astype(F32)
            pool_s[r0:r0 + rc, lanes] = ws / cnt - cur
        mixed = jnp.dot(pool_s[:, lanes].astype(BF16), pw_ref[gi].astype(BF16),
                        preferred_element_type=F32)
        mid_ref[:, cw + gi * gw:cw + (gi + 1) * gw] = (mixed * ps_ref[:, lanes]).astype(mid_ref.dtype)


def mixer_mid(z, row0, n_seq, seq_len, tt, halo_arrays, halo_map, halo_glu, zero_first, pos0,
              conv_w, conv_b, ln_g, ln_b, pool_w, pool_scale):
    ck, cw = conv_w.shape
    pw = pool_scale.shape[0]
    assert cw == pw and seq_len % tt == 0 and tt >= HALO and row0 % tt == 0 and ck - 1 <= HALO
    nt = seq_len // tt
    base = row0 // tt

    def cur_map(col):
        return lambda b, t: (base + b * nt + t, col)

    ha, hg, hp = halo_arrays
    in_specs = [pl.BlockSpec((tt, cw), cur_map(0)),
                pl.BlockSpec((tt, cw), cur_map(1)),
                pl.BlockSpec((tt, pw), cur_map(2)),
                pl.BlockSpec((HALO, cw), halo_map(0)),
                pl.BlockSpec((HALO, cw), halo_map(1)),
                pl.BlockSpec((HALO, pw), halo_map(2)),
                pl.BlockSpec((ck, cw), lambda b, t: (0, 0)),
                pl.BlockSpec((1, cw), lambda b, t: (0, 0)),
                pl.BlockSpec((1, cw), lambda b, t: (0, 0)),
                pl.BlockSpec((1, cw), lambda b, t: (0, 0)),
                pl.BlockSpec(pool_w.shape, lambda b, t: (0, 0, 0)),
                pl.BlockSpec((1, pw), lambda b, t: (0, 0))]
    return pl.pallas_call(
        functools.partial(_mixer_mid_kernel, halo_glu=halo_glu, zero_first=zero_first, pos0=pos0),
        grid=(n_seq, nt),
        in_specs=in_specs,
        out_specs=[pl.BlockSpec((tt, cw + pw), lambda b, t: (b * nt + t, 0)),
                   pl.BlockSpec((1, HALO, cw), lambda b, t: (b, 0, 0))],
        out_shape=[jax.ShapeDtypeStruct((n_seq * seq_len, cw + pw), BF16),
                   jax.ShapeDtypeStruct((n_seq, HALO, cw), F32)],
        scratch_shapes=[pltpu.VMEM((HALO + tt, cw), F32), pltpu.VMEM((HALO + tt, pw), F32),
                        pltpu.VMEM((tt, cw), F32), pltpu.VMEM((tt, pw), F32)],
        compiler_params=_params("parallel", "arbitrary"),
        name="mixer_mid",
    )(z, z, z, ha, hg, hp, conv_w, conv_b.reshape(1, cw), ln_g.reshape(1, cw), ln_b.reshape(1, cw),
      pool_w, pool_scale.reshape(1, pw))


def _spatial_kernel(u_ref, v_ref, lng_ref, lnb_ref, ws_ref, bs_ref, o_ref, vn_ref, *, n_prompt_blocks, sub):
    i = pl.program_id(0)
    blk = u_ref.shape[0]
    n_heads = ws_ref.shape[1]
    hd = u_ref.shape[1] // n_heads

    v = v_ref[...]
    mu = jnp.mean(v, axis=-1, keepdims=True)
    vc = v - mu
    var = jnp.mean(vc * vc, axis=-1, keepdims=True)
    vn = vc * lax.rsqrt(var + LN_EPS) * lng_ref[...] + lnb_ref[...]

    @pl.when(i >= n_prompt_blocks)
    def _():
        vn_ref[...] = vn

    vb = vn.astype(BF16)
    ri = lax.broadcasted_iota(jnp.int32, (blk, blk), 0)
    ci = lax.broadcasted_iota(jnp.int32, (blk, blk), 1)
    is_sample = i >= n_prompt_blocks
    rq = jnp.where(is_sample, ri // sub, ri // CHUNK)
    cq = jnp.where(is_sample, ci // sub, ci // CHUNK)
    mask = jnp.logical_and(cq <= rq, cq >= jnp.where(is_sample, rq, 0))
    for h in range(n_heads):
        w = jnp.where(mask, ws_ref[0, h], 0.0).astype(BF16)
        s = jnp.dot(w, vb[:, h * hd:(h + 1) * hd], preferred_element_type=F32) + bs_ref[0, :, h:h + 1]
        o_ref[:, h * hd:(h + 1) * hd] = (u_ref[:, h * hd:(h + 1) * hd] * s).astype(o_ref.dtype)


def spatial_gate(z, n_prompt_rows, sub, ln_g, ln_b, w_all, b_all):
    m, two_gw = z.shape
    gw = two_gw // 2
    blk = GMLP_BLOCK
    assert m % blk == 0 and n_prompt_rows % blk == 0
    npb = n_prompt_rows // blk
    nblk = m // blk
    return pl.pallas_call(
        functools.partial(_spatial_kernel, n_prompt_blocks=npb, sub=sub),
        grid=(nblk,),
        in_specs=[pl.BlockSpec((blk, gw), lambda i: (i, 0)),
                  pl.BlockSpec((blk, gw), lambda i: (i, 1)),
                  pl.BlockSpec((1, gw), lambda i: (0, 0)),
                  pl.BlockSpec((1, gw), lambda i: (0, 0)),
                  pl.BlockSpec((1,) + w_all.shape[1:], lambda i: (jnp.where(i >= npb, 1, 0), 0, 0, 0)),
                  pl.BlockSpec((1,) + b_all.shape[1:], lambda i: (jnp.where(i >= npb, 1, 0), 0, 0))],
        out_specs=[pl.BlockSpec((blk, gw), lambda i: (i, 0)),
                   pl.BlockSpec((blk, gw), lambda i: (jnp.maximum(i - npb, 0), 0))],
        out_shape=[jax.ShapeDtypeStruct((m, gw), BF16),
                   jax.ShapeDtypeStruct((m - n_prompt_rows, gw), F32)],
        compiler_params=_params("arbitrary"),
        name="spatial_gate",
    )(z, z, ln_g.reshape(1, gw), ln_b.reshape(1, gw), w_all, b_all)


def _router_kernel(x_ref, g_ref, wr_ref, xn_ref, idx_ref, gate_ref):
    xn = _rms(x_ref[...], g_ref[...])
    xn_ref[...] = xn
    n_e = wr_ref.shape[0]
    tb = x_ref.shape[0]
    logits = [jnp.sum(xn * wr_ref[e:e + 1, :], axis=-1, keepdims=True) for e in range(n_e)]
    m1 = logits[0]
    i1 = jnp.zeros((tb, 1), jnp.int32)
    for e in range(1, n_e):
        better = logits[e] > m1
        m1 = jnp.where(better, logits[e], m1)
        i1 = jnp.where(better, e, i1)
    m2 = jnp.full((tb, 1), -jnp.inf, F32)
    i2 = jnp.zeros((tb, 1), jnp.int32)
    for e in range(n_e):
        better = jnp.logical_and(i1 != e, logits[e] > m2)
        m2 = jnp.where(better, logits[e], m2)
        i2 = jnp.where(better, e, i2)
    ex = jnp.exp(m2 - m1)
    den = 1.0 + ex
    lane = lax.broadcasted_iota(jnp.int32, idx_ref.shape, 1)
    idx_ref[...] = jnp.where(lane == 0, i1, jnp.where(lane == 1, i2, 0))
    gate_ref[...] = jnp.where(lane == 0, 1.0 / den, jnp.where(lane == 1, ex / den, 0.0))


def router(x, norm_g, w_router):
    m, d = x.shape
    n_e = w_router.shape[1]
    tb = _pick(m, 256, 8)
    return pl.pallas_call(
        _router_kernel,
        grid=(m // tb,),
        in_specs=[pl.BlockSpec((tb, d), lambda i: (i, 0)),
                  pl.BlockSpec((1, d), lambda i: (0, 0)),
                  pl.BlockSpec((n_e, d), lambda i: (0, 0))],
        out_specs=[pl.BlockSpec((tb, d), lambda i: (i, 0)),
                   pl.BlockSpec((tb, 128), lambda i: (i, 0)),
                   pl.BlockSpec((tb, 128), lambda i: (i, 0))],
        out_shape=[jax.ShapeDtypeStruct((m, d), F32),
                   jax.ShapeDtypeStruct((m, 128), jnp.int32),
                   jax.ShapeDtypeStruct((m, 128), F32)],
        compiler_params=_params("parallel"),
        name="router",
    )(x, norm_g.reshape(1, d), w_router.T)


def _gather_kernel(idx_ref, x_hbm, o_ref, buf, sem):
    i = pl.program_id(0)
    tg = buf.shape[0]

    def issue(r, carry):
        tok = idx_ref[i * tg + r]
        pltpu.make_async_copy(x_hbm.at[pl.ds(tok, 1), :], buf.at[pl.ds(r, 1), :], sem).start()
        return carry

    lax.fori_loop(0, tg, issue, 0)
    pltpu.make_async_copy(x_hbm.at[pl.ds(0, tg), :], buf, sem).wait()
    o_ref[...] = buf[...].astype(o_ref.dtype)


def gather_rows(x, idx, out_dtype):
    n_out = idx.shape[0]
    d = x.shape[1]
    tg = _pick(n_out, 512, 16)
    assert x.shape[0] >= tg
    return pl.pallas_call(
        _gather_kernel,
        grid_spec=pltpu.PrefetchScalarGridSpec(
            num_scalar_prefetch=1,
            grid=(n_out // tg,),
            in_specs=[pl.BlockSpec(memory_space=pl.ANY)],
            out_specs=pl.BlockSpec((tg, d), lambda i, idx: (i, 0)),
            scratch_shapes=[pltpu.VMEM((tg, d), F32), pltpu.SemaphoreType.DMA(())],
        ),
        out_shape=jax.ShapeDtypeStruct((n_out, d), out_dtype),
        compiler_params=_params("arbitrary"),
        name="gather_rows",
    )(idx, x)


def _combine_kernel(pos_ref, rows_hbm, gate_ref, r_ref, g_ref, x_ref, h_ref, buf, sem):
    i = pl.program_id(0)
    tb = r_ref.shape[0]

    def issue(r, carry):
        for k in range(TOP_K):
            p = pos_ref[(i * tb + r) * TOP_K + k]
            pltpu.make_async_copy(rows_hbm.at[pl.ds(p, 1), :], buf.at[k, pl.ds(r, 1), :], sem).start()
        return carry

    lax.fori_loop(0, tb, issue, 0)
    for k in range(TOP_K):
        pltpu.make_async_copy(rows_hbm.at[pl.ds(0, tb), :], buf.at[k], sem).wait()
    y = gate_ref[:, 0:1] * buf[0] + gate_ref[:, 1:2] * buf[1]
    x = r_ref[...] + y
    x_ref[...] = x
    h_ref[...] = _rms(x, g_ref[...]).astype(h_ref.dtype)


def combine(rows, pos, gate, resid, norm_g, norm_dtype):
    m, d = resid.shape
    tb = _pick(m, 256, 16)
    assert rows.shape[0] >= tb
    return pl.pallas_call(
        _combine_kernel,
        grid_spec=pltpu.PrefetchScalarGridSpec(
            num_scalar_prefetch=1,
            grid=(m // tb,),
            in_specs=[pl.BlockSpec(memory_space=pl.ANY),
                      pl.BlockSpec((tb, 128), lambda i, pos: (i, 0)),
                      pl.BlockSpec((tb, d), lambda i, pos: (i, 0)),
                      pl.BlockSpec((1, d), lambda i, pos: (0, 0))],
            out_specs=[pl.BlockSpec((tb, d), lambda i, pos: (i, 0)),
                       pl.BlockSpec((tb, d), lambda i, pos: (i, 0))],
            scratch_shapes=[pltpu.VMEM((TOP_K, tb, d), F32), pltpu.SemaphoreType.DMA(())],
        ),
        out_shape=[jax.ShapeDtypeStruct((m, d), F32), jax.ShapeDtypeStruct((m, d), norm_dtype)],
        compiler_params=_params("arbitrary"),
        name="combine",
    )(pos.reshape(-1), rows, gate, resid, norm_g.reshape(1, d))


def _moe_layout(top_e, n_experts, tm):
    n_tok = top_e.shape[0]
    n_asg = n_tok * TOP_K
    flat_e = top_e.reshape(-1)
    onehot = (flat_e[:, None] == jnp.arange(n_experts)[None, :]).astype(jnp.int32)
    rank = jnp.sum((jnp.cumsum(onehot, axis=0) - onehot) * onehot, axis=1)
    counts = jnp.sum(onehot, axis=0)
    padded = (counts + tm - 1) // tm * tm
    pends = jnp.cumsum(padded)
    pstarts = pends - padded
    dest = pstarts[flat_e] + rank
    n_blocks = -(-n_asg // tm) + n_experts
    n_slots = n_blocks * tm
    src_tok = jnp.zeros((n_slots,), jnp.int32).at[dest].set(jnp.arange(n_asg, dtype=jnp.int32) // TOP_K)
    bstart = jnp.arange(n_blocks, dtype=jnp.int32) * tm
    be = jnp.minimum(jnp.searchsorted(pends, bstart, side='right'), n_experts - 1).astype(jnp.int32)
    bv = jnp.clip(counts[be] - (bstart - pstarts[be]), 0, tm).astype(jnp.int32)
    n_used = jnp.maximum(pends[-1] // tm, 1).astype(jnp.int32)
    last = n_used - 1
    live = jnp.arange(n_blocks) < n_used
    be = jnp.where(live, be, be[last])
    bx = jnp.where(live, jnp.arange(n_blocks, dtype=jnp.int32), last)
    bv = jnp.where(live, bv, 0)
    return dest.reshape(n_tok, TOP_K).astype(jnp.int32), src_tok, be, bv, bx, n_blocks


def _even_layer(x, h, np_rows, dims, state_conv, state_pool, w_in, conv_w, conv_b, ln_g, ln_b,
                pool_w, pool_scale, w_out, ffn_norm, w_gate, w_up, w_down, next_norm):
    n_b, seq, d_b, d_seq = dims
    m = x.shape[0]
    ck, cw = conv_w.shape
    pw = pool_scale.shape[0]
    z = matmul_act(h, w_in)
    tt = _pick(seq, 256, HALO)

    def prompt_halo(col):
        return lambda b, t: (jnp.maximum((b * seq + t * tt) // HALO - 1, 0), col)

    mid_p, cl_p = mixer_mid(z, 0, n_b, seq, tt, (z, z, z), prompt_halo, True, True, 0,
                            conv_w, conv_b, ln_g, ln_b, pool_w, pool_scale)
    sc = jnp.pad(state_conv, ((0, 0), (HALO - (ck - 1), 0), (0, 0))).reshape(d_b * HALO, cw)
    sp = jnp.pad(state_pool, ((0, 0), (HALO - state_pool.shape[1], 0), (0, 0))).reshape(d_b * HALO, pw)

    def sample_halo(col):
        return lambda b, t: (b, 0)

    mid_s, cl_s = mixer_mid(z, np_rows, d_b, d_seq, d_seq, (sc, sc, sp), sample_halo, False, False,
                            PAST_LEN, conv_w, conv_b, ln_g, ln_b, pool_w, pool_scale)
    mid = jnp.concatenate([mid_p, mid_s], axis=0)
    x, h = matmul_resid(mid, w_out, x, ffn_norm)
    nb = m // _pick(m, 1024, 16)
    ar = jnp.arange(nb, dtype=jnp.int32)
    x, h = swiglu_grouped(h, w_gate[None], w_up[None], w_down[None], jnp.zeros((nb,), jnp.int32),
                          jnp.full((nb,), m // nb, jnp.int32), ar, resid=x, norm_g=next_norm)
    n_hist = state_pool.shape[1]
    xp_p = z[:np_rows, 2 * cw:].reshape(n_b, seq, pw)[:, seq - n_hist:]
    xp_s = z[np_rows:, 2 * cw:].reshape(d_b, d_seq, pw)[:, d_seq - n_hist:]
    new_state = (cl_p[:, HALO - (ck - 1):], xp_p, cl_s[:, HALO - (ck - 1):], xp_s)
    return x, h, new_state


def _odd_layer(x, h, np_rows, dims, w_in, b_in, ln_g, ln_b, w_s, b_s, w_out, ffn_norm, w_router,
               w_gate, w_up, w_down, next_norm, next_dtype):
    n_b, seq, d_b, d_seq = dims
    m, d = x.shape
    n_e = w_router.shape[1]
    z = matmul_act(h, w_in, bias=b_in, gelu=True)
    gw = z.shape[1] // 2
    rep = GMLP_BLOCK // d_seq
    w_all = jnp.stack([w_s, jnp.tile(w_s[:, :d_seq, :d_seq], (1, rep, rep))])
    b_all = jnp.stack([jnp.transpose(b_s), jnp.transpose(jnp.tile(b_s[:, :d_seq], (1, rep)))])
    b_all = jnp.pad(b_all, ((0, 0), (0, 0), (0, 128 - b_all.shape[2])))
    gated, vn_s = spatial_gate(z, np_rows, d_seq, ln_g, ln_b, w_all, b_all)
    x, _ = matmul_resid(gated, w_out, x)
    xn, idx, gate = router(x, ffn_norm, w_router)
    tm = _pick(m, 1024, 16)
    pos, src_tok, be, bv, bx, _ = _moe_layout(idx[:, :TOP_K], n_e, tm)
    xs = gather_rows(xn, src_tok, BF16)
    rows = swiglu_grouped(xs, w_gate, w_up, w_down, be, bv, bx)
    x, h = combine(rows, pos, gate, x, next_norm, next_dtype)
    return x, h, vn_s.reshape(d_b, d_seq, gw)


def kernel(x_prompt, x_sample, state_conv, state_pool, even_mix_norm, even_w_in, even_conv_w, even_conv_b, even_conv_ln_g, even_conv_ln_b, even_pool_w, even_pool_scale, even_w_out, even_ffn_norm, even_ffn_w_gate, even_ffn_w_up, even_ffn_w_down, odd_mix_norm, odd_w_in, odd_b_in, odd_v_ln_g, odd_v_ln_b, odd_w_s, odd_b_s, odd_w_out, odd_ffn_norm, odd_router_w, odd_moe_w_gate, odd_moe_w_up, odd_moe_w_down, final_norm):
    n_b, seq, d = x_prompt.shape
    d_b, d_seq, _ = x_sample.shape
    assert GMLP_BLOCK % d_seq == 0 and d_seq <= CHUNK and seq % GMLP_BLOCK == 0 and d_seq >= HALO
    dims = (n_b, seq, d_b, d_seq)
    np_rows = n_b * seq
    depth = even_w_in.shape[0] + odd_w_in.shape[0]
    x = jnp.concatenate([x_prompt.reshape(np_rows, d), x_sample.reshape(d_b * d_seq, d)], axis=0)
    h = rmsnorm_cast(x, even_mix_norm[0], BF16)
    conv_p, pool_p, conv_s, pool_s, v_s = [], [], [], [], []
    for layer in range(depth):
        li = layer // 2
        if layer % 2 == 0:
            nxt = odd_mix_norm[li] if layer + 1 < depth else final_norm
            x, h, st = _even_layer(
                x, h, np_rows, dims, state_conv[li], state_pool[li], even_w_in[li], even_conv_w[li],
                even_conv_b[li], even_conv_ln_g[li], even_conv_ln_b[li], even_pool_w[li],
                even_pool_scale[li], even_w_out[li], even_ffn_norm[li], even_ffn_w_gate[li],
                even_ffn_w_up[li], even_ffn_w_down[li], nxt)
            conv_p.append(st[0])
            pool_p.append(st[1])
            conv_s.append(st[2])
            pool_s.append(st[3])
        else:
            last = layer + 1 == depth
            nxt = final_norm if last else even_mix_norm[li + 1]
            x, h, vs = _odd_layer(
                x, h, np_rows, dims, odd_w_in[li], odd_b_in[li], odd_v_ln_g[li], odd_v_ln_b[li],
                odd_w_s[li], odd_b_s[li], odd_w_out[li], odd_ffn_norm[li], odd_router_w[li],
                odd_moe_w_gate[li], odd_moe_w_up[li], odd_moe_w_down[li], nxt, F32 if last else BF16)
            v_s.append(vs)
    assert depth % 2 == 0, "the final rmsnorm is fused into the last expert combine"
    y = h
    return (y[:np_rows].reshape(n_b, seq, d), y[np_rows:].reshape(d_b, d_seq, d),
            jnp.stack(conv_p), jnp.stack(pool_p), jnp.stack(conv_s), jnp.stack(pool_s), jnp.stack(v_s))
```

```python
import functools
import math

import jax
import jax.numpy as jnp
from jax import lax
from jax.experimental import pallas as pl
from jax.experimental.pallas import tpu as pltpu

F32 = jnp.float32
BF16 = jnp.bfloat16

CHUNK = 64
GMLP_BLOCK = 128
POOL_WINDOWS = (2, 4, 8, 16)
PAST_LEN = 1024
TOP_K = 2
RMS_EPS = 1e-6
LN_EPS = 1e-5

HALO = 32
VMEM_LIMIT = 60000 * 1024
VMEM_BUDGET = 50 * 1024 * 1024
ROW_TILE = 1024
SUB_ROWS = 256


def _pick(n, target, mult):
    best = None
    for t in range(mult, min(n, target) + 1, mult):
        if n % t == 0:
            best = t
    assert best is not None, (n, target, mult)
    return best


def _params(*sem):
    return pltpu.CompilerParams(dimension_semantics=sem, vmem_limit_bytes=VMEM_LIMIT)


def _rms(x, g):
    return x * lax.rsqrt(jnp.mean(x * x, axis=-1, keepdims=True) + RMS_EPS) * g


def _sigmoid(x):
    return 1.0 / (1.0 + jnp.exp(-x))


def _col_tile(n, fits):
    for tn in (1024, 512, 256, 128):
        if n % tn == 0 and fits(tn):
            return tn
    raise ValueError(f"no column tile of {n} fits VMEM")


def _dense_kernel(*refs, norm, n_w, act, has_bias, has_resid):
    it = iter(refs)
    a_ref = next(it)
    g_ref = next(it) if norm else None
    w_refs = [next(it) for _ in range(n_w)]
    b_ref = next(it) if has_bias else None
    r_ref = next(it) if has_resid else None
    o_ref = next(it)
    if norm:
        h_s = next(it)

        @pl.when(pl.program_id(1) == 0)
        def _():
            h_s[...] = _rms(a_ref[...], g_ref[...]).astype(BF16)

        a = h_s[...]
    else:
        a = a_ref[...]
    acc = jnp.dot(a, w_refs[0][...].astype(BF16), preferred_element_type=F32)
    if has_bias:
        acc = acc + b_ref[...]
    if act == "gelu":
        acc = 0.5 * acc * (1.0 + lax.erf(acc * (1.0 / math.sqrt(2.0))))
    elif act == "swiglu":
        up = jnp.dot(a, w_refs[1][...].astype(BF16), preferred_element_type=F32)
        acc = acc * _sigmoid(acc) * up
    if has_resid:
        acc = r_ref[...] + acc
    o_ref[...] = acc.astype(o_ref.dtype)


def dense_matmul(a, w, li, *, norm_g=None, w2=None, bias=None, act=None, resid=None, out_dtype=F32,
                 name="dense_matmul"):
    m, k = a.shape
    n = w.shape[-1]
    norm = norm_g is not None
    n_w = 2 if w2 is not None else 1
    tm = _pick(m, ROW_TILE, 16)
    ab = a.dtype.itemsize
    ob = jnp.dtype(out_dtype).itemsize

    def est(tn, a_bufs):
        e = tm * k * ab * a_bufs + (tm * k * 2 if norm else 0)
        e += n_w * k * tn * (4 * 2 + 2)
        e += n_w * tm * tn * 4
        e += tm * tn * ob * 2 + (tm * tn * 4 * 2 if resid is not None else 0)
        return e

    a_bufs = 2
    try:
        tn = _col_tile(n, lambda t: est(t, 2) <= VMEM_BUDGET)
    except ValueError:
        a_bufs = 1
        tn = _col_tile(n, lambda t: est(t, 1) <= VMEM_BUDGET)
    a_mode = {} if a_bufs == 2 else {"pipeline_mode": pl.Buffered(1)}
    in_specs = [pl.BlockSpec((tm, k), lambda i, j: (i, 0), **a_mode)]
    args = [a]
    if norm:
        in_specs.append(pl.BlockSpec((1, k), lambda i, j: (0, 0)))
        args.append(norm_g.reshape(1, k))
    for ww in (w, w2)[:n_w]:
        in_specs.append(pl.BlockSpec((None, k, tn), lambda i, j: (li, 0, j)))
        args.append(ww)
    if bias is not None:
        in_specs.append(pl.BlockSpec((None, 1, tn), lambda i, j: (li, 0, j)))
        args.append(bias.reshape(bias.shape[0], 1, n))
    if resid is not None:
        in_specs.append(pl.BlockSpec((tm, tn), lambda i, j: (i, j)))
        args.append(resid)
    return pl.pallas_call(
        functools.partial(_dense_kernel, norm=norm, n_w=n_w, act=act, has_bias=bias is not None,
                          has_resid=resid is not None),
        grid=(m // tm, n // tn),
        in_specs=in_specs,
        out_specs=pl.BlockSpec((tm, tn), lambda i, j: (i, j)),
        out_shape=jax.ShapeDtypeStruct((m, n), out_dtype),
        scratch_shapes=[pltpu.VMEM((tm, k), BF16)] if norm else [],
        compiler_params=_params("parallel", "arbitrary"),
        name=name,
    )(*args)


def _grouped_kernel(be_ref, bv_ref, bx_ref, a_ref, *refs, n_w):
    w_refs, o_ref = refs[:n_w], refs[n_w]
    nv = bv_ref[pl.program_id(0)]
    tm = a_ref.shape[0]

    def compute(rows):
        a = a_ref[rows, :]
        acc = jnp.dot(a, w_refs[0][...].astype(BF16), preferred_element_type=F32)
        if n_w == 2:
            up = jnp.dot(a, w_refs[1][...].astype(BF16), preferred_element_type=F32)
            acc = acc * _sigmoid(acc) * up
        o_ref[rows, :] = acc.astype(o_ref.dtype)

    @pl.when(nv == tm)
    def _():
        compute(slice(0, tm))

    @pl.when(jnp.logical_and(nv > 0, nv < tm))
    def _():
        for s in range(tm // SUB_ROWS):
            rows = slice(s * SUB_ROWS, (s + 1) * SUB_ROWS)

            @pl.when(s * SUB_ROWS < nv)
            def _():
                compute(rows)

            @pl.when(s * SUB_ROWS >= nv)
            def _():
                o_ref[rows, :] = jnp.zeros((SUB_ROWS, o_ref.shape[1]), o_ref.dtype)

    @pl.when(nv == 0)
    def _():
        o_ref[...] = jnp.zeros(o_ref.shape, o_ref.dtype)


def grouped_matmul(a, w, block_e, block_valid, block_x, *, w2=None, out_dtype=F32, name="grouped_matmul"):
    m, k = a.shape
    n = w.shape[-1]
    nb = block_e.shape[0]
    assert m == nb * ROW_TILE
    tm = ROW_TILE
    n_w = 2 if w2 is not None else 1
    ob = jnp.dtype(out_dtype).itemsize

    def est(tn):
        return (tm * k * 2 * 2 + n_w * k * tn * (4 * 2 + 2) + n_w * tm * tn * 4 + tm * tn * ob * 2)

    tn = _col_tile(n, lambda t: t <= 512 and est(t) <= VMEM_BUDGET)
    nj = n // tn

    def a_map(b, j, be, bv, bx):
        return (bx[b], 0)

    def w_map(b, j, be, bv, bx):
        return (be[b], 0, jnp.where(bv[b] > 0, j, nj - 1))

    in_specs = [pl.BlockSpec((tm, k), a_map)]
    in_specs += [pl.BlockSpec((None, k, tn), w_map)] * n_w
    return pl.pallas_call(
        functools.partial(_grouped_kernel, n_w=n_w),
        grid_spec=pltpu.PrefetchScalarGridSpec(
            num_scalar_prefetch=3,
            grid=(nb, nj),
            in_specs=in_specs,
            out_specs=pl.BlockSpec((tm, tn), lambda b, j, be, bv, bx: (b, j)),
        ),
        out_shape=jax.ShapeDtypeStruct((m, n), out_dtype),
        compiler_params=_params("arbitrary", "arbitrary"),
        name=name,
    )(block_e, block_valid, block_x, a, *((w, w2)[:n_w]))


def _mixer_mid_kernel(a_ref, g_ref, p_ref, ha_ref, hg_ref, hp_ref, cw_ref, cb_ref, lng_ref, lnb_ref,
                      pw_ref, ps_ref, mid_ref, clast_ref, cx_s, px_s, y_s, pool_s,
                      *, halo_glu, zero_first, pos0):
    t = pl.program_id(1)
    tt, cw = a_ref.shape
    pw = p_ref.shape[1]
    ck = cw_ref.shape[0]
    off = HALO - (ck - 1)

    c = a_ref[...] * _sigmoid(g_ref[...])
    ch = ha_ref[...]
    if halo_glu:
        ch = ch * _sigmoid(hg_ref[...])
    ph = hp_ref[...]
    if zero_first:
        ch = jnp.where(t > 0, ch, 0.0)
        ph = jnp.where(t > 0, ph, 0.0)
    cx_s[0:HALO, :] = ch
    cx_s[HALO:HALO + tt, :] = c
    px_s[0:HALO, :] = ph
    px_s[HALO:HALO + tt, :] = p_ref[...]

    @pl.when(t == pl.num_programs(1) - 1)
    def _():
        clast_ref[0] = c[tt - HALO:tt, :]

    rc = min(tt, 32)
    lc = min(cw, 512)
    for r0 in range(0, tt, rc):
        for l0 in range(0, cw, lc):
            acc = jnp.broadcast_to(cb_ref[:, l0:l0 + lc], (rc, lc))
            for k in range(ck):
                acc = acc + cw_ref[k:k + 1, l0:l0 + lc] * cx_s[r0 + off + k:r0 + off + k + rc, l0:l0 + lc]
            y_s[r0:r0 + rc, l0:l0 + lc] = acc

    y = y_s[...]
    mu = jnp.mean(y, axis=-1, keepdims=True)
    yc = y - mu
    var = jnp.mean(yc * yc, axis=-1, keepdims=True)
    yn = yc * lax.rsqrt(var + LN_EPS) * lng_ref[...] + lnb_ref[...]
    mid_ref[:, 0:cw] = (yn * _sigmoid(yn)).astype(mid_ref.dtype)

    n_grp = len(POOL_WINDOWS)
    gw = pw // n_grp
    for gi, win in enumerate(POOL_WINDOWS):
        lanes = slice(gi * gw, (gi + 1) * gw)
        for r0 in range(0, tt, rc):
            cur = px_s[HALO + r0:HALO + r0 + rc, lanes]
            ws = cur
            for i in range(1, win):
                ws = ws + px_s[HALO + r0 - i:HALO + r0 - i + rc, lanes]
            pos = pos0 + t * tt + r0 + lax.broadcasted_iota(jnp.int32, (rc, 1), 0)
            cnt = jnp.minimum(pos + 1, win).astype(F32)
            pool_s[r0:r0 + rc, lanes] = ws / cnt - cur
        mixed = jnp.dot(pool_s[:, lanes].astype(BF16), pw_ref[gi].astype(BF16),
                        preferred_element_type=F32)
        mid_ref[:, cw + gi * gw:cw + (gi + 1) * gw] = (mixed * ps_ref[:, lanes]).astype(mid_ref.dtype)


def mixer_mid(z, row0, n_seq, seq_len, tt, halo_arrays, halo_map, halo_glu, zero_first, pos0,
              conv_w, conv_b, ln_g, ln_b, pool_w, pool_scale):
    ck, cw = conv_w.shape
    pw = pool_scale.shape[0]
    assert cw == pw and seq_len % tt == 0 and tt >= HALO and row0 % tt == 0 and ck - 1 <= HALO
    nt = seq_len // tt
    base = row0 // tt

    def cur_map(col):
        return lambda b, t: (base + b * nt + t, col)

    ha, hg, hp = halo_arrays
    in_specs = [pl.BlockSpec((tt, cw), cur_map(0)),
                pl.BlockSpec((tt, cw), cur_map(1)),
                pl.BlockSpec((tt, pw), cur_map(2)),
                pl.BlockSpec((HALO, cw), halo_map(0)),
                pl.BlockSpec((HALO, cw), halo_map(1)),
                pl.BlockSpec((HALO, pw), halo_map(2)),
                pl.BlockSpec((ck, cw), lambda b, t: (0, 0)),
                pl.BlockSpec((1, cw), lambda b, t: (0, 0)),
                pl.BlockSpec((1, cw), lambda b, t: (0, 0)),
                pl.BlockSpec((1, cw), lambda b, t: (0, 0)),
                pl.BlockSpec(pool_w.shape, lambda b, t: (0, 0, 0)),
                pl.BlockSpec((1, pw), lambda b, t: (0, 0))]
    return pl.pallas_call(
        functools.partial(_mixer_mid_kernel, halo_glu=halo_glu, zero_first=zero_first, pos0=pos0),
        grid=(n_seq, nt),
        in_specs=in_specs,
        out_specs=[pl.BlockSpec((tt, cw + pw), lambda b, t: (b * nt + t, 0)),
                   pl.BlockSpec((1, HALO, cw), lambda b, t: (b, 0, 0))],
        out_shape=[jax.ShapeDtypeStruct((n_seq * seq_len, cw + pw), BF16),
                   jax.ShapeDtypeStruct((n_seq, HALO, cw), F32)],
        scratch_shapes=[pltpu.VMEM((HALO + tt, cw), F32), pltpu.VMEM((HALO + tt, pw), F32),
                        pltpu.VMEM((tt, cw), F32), pltpu.VMEM((tt, pw), F32)],
        compiler_params=_params("parallel", "arbitrary"),
        name="mixer_mid",
    )(z, z, z, ha, hg, hp, conv_w, conv_b.reshape(1, cw), ln_g.reshape(1, cw), ln_b.reshape(1, cw),
      pool_w, pool_scale.reshape(1, pw))


def _spatial_kernel(u_ref, v_ref, lng_ref, lnb_ref, ws_ref, bs_ref, o_ref, vn_ref, *, n_prompt_blocks, sub):
    i = pl.program_id(0)
    blk = u_ref.shape[0]
    n_heads = ws_ref.shape[1]
    hd = u_ref.shape[1] // n_heads

    v = v_ref[...]
    mu = jnp.mean(v, axis=-1, keepdims=True)
    vc = v - mu
    var = jnp.mean(vc * vc, axis=-1, keepdims=True)
    vn = vc * lax.rsqrt(var + LN_EPS) * lng_ref[...] + lnb_ref[...]

    @pl.when(i >= n_prompt_blocks)
    def _():
        vn_ref[...] = vn

    vb = vn.astype(BF16)
    ri = lax.broadcasted_iota(jnp.int32, (blk, blk), 0)
    ci = lax.broadcasted_iota(jnp.int32, (blk, blk), 1)
    is_sample = i >= n_prompt_blocks
    rq = jnp.where(is_sample, ri // sub, ri // CHUNK)
    cq = jnp.where(is_sample, ci // sub, ci // CHUNK)
    mask = jnp.logical_and(cq <= rq, cq >= jnp.where(is_sample, rq, 0))
    for h in range(n_heads):
        w = jnp.where(mask, ws_ref[0, h], 0.0).astype(BF16)
        s = jnp.dot(w, vb[:, h * hd:(h + 1) * hd], preferred_element_type=F32) + bs_ref[0, :, h:h + 1]
        o_ref[:, h * hd:(h + 1) * hd] = (u_ref[:, h * hd:(h + 1) * hd] * s).astype(o_ref.dtype)


def spatial_gate(z, n_prompt_rows, sub, ln_g, ln_b, w_all, b_all):
    m, two_gw = z.shape
    gw = two_gw // 2
    blk = GMLP_BLOCK
    assert m % blk == 0 and n_prompt_rows % blk == 0
    npb = n_prompt_rows // blk
    nblk = m // blk
    return pl.pallas_call(
        functools.partial(_spatial_kernel, n_prompt_blocks=npb, sub=sub),
        grid=(nblk,),
        in_specs=[pl.BlockSpec((blk, gw), lambda i: (i, 0)),
                  pl.BlockSpec((blk, gw), lambda i: (i, 1)),
                  pl.BlockSpec((1, gw), lambda i: (0, 0)),
                  pl.BlockSpec((1, gw), lambda i: (0, 0)),
                  pl.BlockSpec((1,) + w_all.shape[1:], lambda i: (jnp.where(i >= npb, 1, 0), 0, 0, 0)),
                  pl.BlockSpec((1,) + b_all.shape[1:], lambda i: (jnp.where(i >= npb, 1, 0), 0, 0))],
        out_specs=[pl.BlockSpec((blk, gw), lambda i: (i, 0)),
                   pl.BlockSpec((blk, gw), lambda i: (jnp.maximum(i - npb, 0), 0))],
        out_shape=[jax.ShapeDtypeStruct((m, gw), BF16),
                   jax.ShapeDtypeStruct((m - n_prompt_rows, gw), F32)],
        compiler_params=_params("arbitrary"),
        name="spatial_gate",
    )(z, z, ln_g.reshape(1, gw), ln_b.reshape(1, gw), w_all, b_all)


def _router_kernel(x_ref, g_ref, wr_ref, xn_ref, idx_ref, gate_ref):
    xn = _rms(x_ref[...], g_ref[...])
    xn_ref[...] = xn
    n_e = wr_ref.shape[0]
    tb = x_ref.shape[0]
    logits = [jnp.sum(xn * wr_ref[e:e + 1, :], axis=-1, keepdims=True) for e in range(n_e)]
    m1 = logits[0]
    i1 = jnp.zeros((tb, 1), jnp.int32)
    for e in range(1, n_e):
        better = logits[e] > m1
        m1 = jnp.where(better, logits[e], m1)
        i1 = jnp.where(better, e, i1)
    m2 = jnp.full((tb, 1), -jnp.inf, F32)
    i2 = jnp.zeros((tb, 1), jnp.int32)
    for e in range(n_e):
        better = jnp.logical_and(i1 != e, logits[e] > m2)
        m2 = jnp.where(better, logits[e], m2)
        i2 = jnp.where(better, e, i2)
    ex = jnp.exp(m2 - m1)
    den = 1.0 + ex
    lane = lax.broadcasted_iota(jnp.int32, idx_ref.shape, 1)
    idx_ref[...] = jnp.where(lane == 0, i1, jnp.where(lane == 1, i2, 0))
    gate_ref[...] = jnp.where(lane == 0, 1.0 / den, jnp.where(lane == 1, ex / den, 0.0))


def router(x, norm_g, w_router):
    m, d = x.shape
    n_e = w_router.shape[1]
    tb = _pick(m, 256, 8)
    return pl.pallas_call(
        _router_kernel,
        grid=(m // tb,),
        in_specs=[pl.BlockSpec((tb, d), lambda i: (i, 0)),
                  pl.BlockSpec((1, d), lambda i: (0, 0)),
                  pl.BlockSpec((n_e, d), lambda i: (0, 0))],
        out_specs=[pl.BlockSpec((tb, d), lambda i: (i, 0)),
                   pl.BlockSpec((tb, 128), lambda i: (i, 0)),
                   pl.BlockSpec((tb, 128), lambda i: (i, 0))],
        out_shape=[jax.ShapeDtypeStruct((m, d), F32),
                   jax.ShapeDtypeStruct((m, 128), jnp.int32),
                   jax.ShapeDtypeStruct((m, 128), F32)],
        compiler_params=_params("parallel"),
        name="router",
    )(x, norm_g.reshape(1, d), w_router.T)


def _gather_kernel(idx_ref, bv_ref, x_hbm, o_ref, buf, sem):
    b = pl.program_id(0)
    tg = buf.shape[0]
    cnt = (bv_ref[b] + SUB_ROWS - 1) // SUB_ROWS * SUB_ROWS

    def issue(r, carry):
        tok = idx_ref[b * tg + r]
        pltpu.make_async_copy(x_hbm.at[pl.ds(tok, 1), :], buf.at[pl.ds(r, 1), :], sem).start()
        return carry

    lax.fori_loop(0, cnt, issue, 0)

    @pl.when(cnt > 0)
    def _():
        pltpu.make_async_copy(x_hbm.at[pl.ds(0, cnt), :], buf.at[pl.ds(0, cnt), :], sem).wait()

    for s in range(tg // SUB_ROWS):
        rows = slice(s * SUB_ROWS, (s + 1) * SUB_ROWS)

        @pl.when(s * SUB_ROWS < cnt)
        def _():
            o_ref[rows, :] = buf[rows, :].astype(o_ref.dtype)

        @pl.when(s * SUB_ROWS >= cnt)
        def _():
            o_ref[rows, :] = jnp.zeros((SUB_ROWS, o_ref.shape[1]), o_ref.dtype)


def gather_rows(x, idx, block_valid, out_dtype):
    n_out = idx.shape[0]
    d = x.shape[1]
    tg = ROW_TILE
    assert n_out % tg == 0 and x.shape[0] >= tg
    return pl.pallas_call(
        _gather_kernel,
        grid_spec=pltpu.PrefetchScalarGridSpec(
            num_scalar_prefetch=2,
            grid=(n_out // tg,),
            in_specs=[pl.BlockSpec(memory_space=pl.ANY)],
            out_specs=pl.BlockSpec((tg, d), lambda i, idx, bv: (i, 0)),
            scratch_shapes=[pltpu.VMEM((tg, d), F32), pltpu.SemaphoreType.DMA(())],
        ),
        out_shape=jax.ShapeDtypeStruct((n_out, d), out_dtype),
        compiler_params=_params("arbitrary"),
        name="gather_rows",
    )(idx, block_valid, x)


def _combine_kernel(pos_ref, rows_hbm, gate_ref, r_ref, *refs, final):
    if final:
        g_ref, o_ref, buf, sem = refs
    else:
        o_ref, buf, sem = refs
    i = pl.program_id(0)
    tb = r_ref.shape[0]

    def issue(r, carry):
        for k in range(TOP_K):
            p = pos_ref[(i * tb + r) * TOP_K + k]
            pltpu.make_async_copy(rows_hbm.at[pl.ds(p, 1), :], buf.at[k, pl.ds(r, 1), :], sem).start()
        return carry

    lax.fori_loop(0, tb, issue, 0)
    for k in range(TOP_K):
        pltpu.make_async_copy(rows_hbm.at[pl.ds(0, tb), :], buf.at[k], sem).wait()
    y = gate_ref[:, 0:1] * buf[0] + gate_ref[:, 1:2] * buf[1]
    x = r_ref[...] + y
    o_ref[...] = _rms(x, g_ref[...]) if final else x


def combine(rows, pos, gate, resid, final_g=None):
    m, d = resid.shape
    tb = _pick(m, 256, 16)
    assert rows.shape[0] >= tb
    final = final_g is not None
    in_specs = [pl.BlockSpec(memory_space=pl.ANY),
                pl.BlockSpec((tb, 128), lambda i, pos: (i, 0)),
                pl.BlockSpec((tb, d), lambda i, pos: (i, 0))]
    args = [rows, gate, resid]
    if final:
        in_specs.append(pl.BlockSpec((1, d), lambda i, pos: (0, 0)))
        args.append(final_g.reshape(1, d))
    return pl.pallas_call(
        functools.partial(_combine_kernel, final=final),
        grid_spec=pltpu.PrefetchScalarGridSpec(
            num_scalar_prefetch=1,
            grid=(m // tb,),
            in_specs=in_specs,
            out_specs=pl.BlockSpec((tb, d), lambda i, pos: (i, 0)),
            scratch_shapes=[pltpu.VMEM((TOP_K, tb, d), F32), pltpu.SemaphoreType.DMA(())],
        ),
        out_shape=jax.ShapeDtypeStruct((m, d), F32),
        compiler_params=_params("arbitrary"),
        name="combine",
    )(pos.reshape(-1), *args)


def _moe_layout(top_e, n_experts, first_expert):
    tm = ROW_TILE
    n_tok = top_e.shape[0]
    n_asg = n_tok * TOP_K
    flat_e = top_e.reshape(-1)
    onehot = (flat_e[:, None] == jnp.arange(n_experts)[None, :]).astype(jnp.int32)
    rank = jnp.sum((jnp.cumsum(onehot, axis=0) - onehot) * onehot, axis=1)
    counts = jnp.sum(onehot, axis=0)
    padded = (counts + tm - 1) // tm * tm
    pends = jnp.cumsum(padded)
    pstarts = pends - padded
    dest = pstarts[flat_e] + rank
    n_blocks = -(-n_asg // tm) + n_experts
    n_slots = n_blocks * tm
    src_tok = (jnp.arange(n_slots, dtype=jnp.int32) % n_tok).at[dest].set(
        jnp.arange(n_asg, dtype=jnp.int32) // TOP_K)
    bstart = jnp.arange(n_blocks, dtype=jnp.int32) * tm
    be = jnp.minimum(jnp.sum((pends[None, :] <= bstart[:, None]).astype(jnp.int32), axis=1), n_experts - 1)
    bv = jnp.clip(counts[be] - (bstart - pstarts[be]), 0, tm).astype(jnp.int32)
    n_used = jnp.maximum(pends[-1] // tm, 1).astype(jnp.int32)
    last = n_used - 1
    live = jnp.arange(n_blocks) < n_used
    be = jnp.where(live, be, be[last]).astype(jnp.int32) + first_expert
    bx = jnp.where(live, jnp.arange(n_blocks, dtype=jnp.int32), last)
    bv = jnp.where(live, bv, 0)
    return dest.reshape(n_tok, TOP_K).astype(jnp.int32), src_tok, be, bv, bx


def _even_layer(x, li, mix_norm, np_rows, dims, state_conv, state_pool, w_in, conv_w, conv_b, ln_g, ln_b,
                pool_w, pool_scale, w_out, ffn_norm, w_gate, w_up, w_down):
    n_b, seq, d_b, d_seq = dims
    ck, cw = conv_w.shape
    pw = pool_scale.shape[0]
    z = dense_matmul(x, w_in, li, norm_g=mix_norm, name="even_in")
    tt = _pick(seq, 256, HALO)

    def prompt_halo(col):
        return lambda b, t: (jnp.maximum((b * seq + t * tt) // HALO - 1, 0), col)

    mid_p, cl_p = mixer_mid(z, 0, n_b, seq, tt, (z, z, z), prompt_halo, True, True, 0,
                            conv_w, conv_b, ln_g, ln_b, pool_w, pool_scale)
    sc = jnp.pad(state_conv, ((0, 0), (HALO - (ck - 1), 0), (0, 0))).reshape(d_b * HALO, cw)
    sp = jnp.pad(state_pool, ((0, 0), (HALO - state_pool.shape[1], 0), (0, 0))).reshape(d_b * HALO, pw)

    def sample_halo(col):
        return lambda b, t: (b, 0)

    mid_s, cl_s = mixer_mid(z, np_rows, d_b, d_seq, d_seq, (sc, sc, sp), sample_halo, False, False,
                            PAST_LEN, conv_w, conv_b, ln_g, ln_b, pool_w, pool_scale)
    mid = jnp.concatenate([mid_p, mid_s], axis=0)
    x = dense_matmul(mid, w_out, li, resid=x, name="even_out")
    hmid = dense_matmul(x, w_gate, li, norm_g=ffn_norm, w2=w_up, act="swiglu", out_dtype=BF16,
                        name="ffn_gate_up")
    x = dense_matmul(hmid, w_down, li, resid=x, name="ffn_down")
    n_hist = state_pool.shape[1]
    xp_p = z[:np_rows, 2 * cw:].reshape(n_b, seq, pw)[:, seq - n_hist:]
    xp_s = z[np_rows:, 2 * cw:].reshape(d_b, d_seq, pw)[:, d_seq - n_hist:]
    new_state = (cl_p[:, HALO - (ck - 1):], xp_p, cl_s[:, HALO - (ck - 1):], xp_s)
    return x, new_state


def _odd_layer(x, li, mix_norm, np_rows, dims, w_in, b_in, ln_g, ln_b, w_s, b_s, w_out, ffn_norm, w_router,
               w_gate, w_up, w_down, final_g):
    n_b, seq, d_b, d_seq = dims
    n_e = w_router.shape[1]
    z = dense_matmul(x, w_in, li, norm_g=mix_norm, bias=b_in, act="gelu", name="gmlp_in")
    gw = z.shape[1] // 2
    rep = GMLP_BLOCK // d_seq
    w_all = jnp.stack([w_s, jnp.tile(w_s[:, :d_seq, :d_seq], (1, rep, rep))])
    b_all = jnp.stack([jnp.transpose(b_s), jnp.transpose(jnp.tile(b_s[:, :d_seq], (1, rep)))])
    b_all = jnp.pad(b_all, ((0, 0), (0, 0), (0, 128 - b_all.shape[2])))
    gated, vn_s = spatial_gate(z, np_rows, d_seq, ln_g, ln_b, w_all, b_all)
    x = dense_matmul(gated, w_out, li, resid=x, name="gmlp_out")
    xn, idx, gate = router(x, ffn_norm, w_router)
    pos, src_tok, be, bv, bx = _moe_layout(idx[:, :TOP_K], n_e, li * n_e)
    xs = gather_rows(xn, src_tok, bv, BF16)
    experts = lambda w: w.reshape((-1,) + w.shape[2:])
    hmid = grouped_matmul(xs, experts(w_gate), be, bv, bx, w2=experts(w_up), out_dtype=BF16,
                          name="expert_gate_up")
    rows = grouped_matmul(hmid, experts(w_down), be, bv, bx, name="expert_down")
    x = combine(rows, pos, gate, x, final_g)
    return x, vn_s.reshape(d_b, d_seq, gw)


def kernel(x_prompt, x_sample, state_conv, state_pool, even_mix_norm, even_w_in, even_conv_w, even_conv_b, even_conv_ln_g, even_conv_ln_b, even_pool_w, even_pool_scale, even_w_out, even_ffn_norm, even_ffn_w_gate, even_ffn_w_up, even_ffn_w_down, odd_mix_norm, odd_w_in, odd_b_in, odd_v_ln_g, odd_v_ln_b, odd_w_s, odd_b_s, odd_w_out, odd_ffn_norm, odd_router_w, odd_moe_w_gate, odd_moe_w_up, odd_moe_w_down, final_norm):
    n_b, seq, d = x_prompt.shape
    d_b, d_seq, _ = x_sample.shape
    assert GMLP_BLOCK % d_seq == 0 and d_seq <= CHUNK and seq % GMLP_BLOCK == 0 and d_seq >= HALO
    dims = (n_b, seq, d_b, d_seq)
    np_rows = n_b * seq
    depth = even_w_in.shape[0] + odd_w_in.shape[0]
    assert depth % 2 == 0, "the final rmsnorm is fused into the last expert combine"
    x = jnp.concatenate([x_prompt.reshape(np_rows, d), x_sample.reshape(d_b * d_seq, d)], axis=0)
    conv_p, pool_p, conv_s, pool_s, v_s = [], [], [], [], []
    for layer in range(depth):
        li = layer // 2
        if layer % 2 == 0:
            x, st = _even_layer(
                x, li, even_mix_norm[li], np_rows, dims, state_conv[li], state_pool[li], even_w_in,
                even_conv_w[li], even_conv_b[li], even_conv_ln_g[li], even_conv_ln_b[li], even_pool_w[li],
                even_pool_scale[li], even_w_out, even_ffn_norm[li], even_ffn_w_gate, even_ffn_w_up,
                even_ffn_w_down)
            conv_p.append(st[0])
            pool_p.append(st[1])
            conv_s.append(st[2])
            pool_s.append(st[3])
        else:
            x, vs = _odd_layer(
                x, li, odd_mix_norm[li], np_rows, dims, odd_w_in, odd_b_in, odd_v_ln_g[li], odd_v_ln_b[li],
                odd_w_s[li], odd_b_s[li], odd_w_out, odd_ffn_norm[li], odd_router_w[li],
                odd_moe_w_gate, odd_moe_w_up, odd_moe_w_down, final_norm if layer + 1 == depth else None)
            v_s.append(vs)
    return (x[:np_rows].reshape(n_b, seq, d), x[np_rows:].reshape(d_b, d_seq, d),
            jnp.stack(conv_p), jnp.stack(pool_p), jnp.stack(conv_s), jnp.stack(pool_s), jnp.stack(v_s))
```

```python
import functools
import math

import jax
import jax.numpy as jnp
from jax import lax
from jax.experimental import pallas as pl
from jax.experimental.pallas import tpu as pltpu

F32 = jnp.float32
BF16 = jnp.bfloat16

CHUNK = 64
GMLP_BLOCK = 128
POOL_WINDOWS = (2, 4, 8, 16)
PAST_LEN = 1024
TOP_K = 2
RMS_EPS = 1e-6
LN_EPS = 1e-5

HALO = 32
VMEM_LIMIT = 60000 * 1024
VMEM_BUDGET = 50 * 1024 * 1024
ROW_TILE = 1024
MAX_ROW_TILE = 1536
SUB_ROWS = 256
ISSUE_UNROLL = 8


def _pick(n, target, mult):
    best = None
    for t in range(mult, min(n, target) + 1, mult):
        if n % t == 0:
            best = t
    assert best is not None, (n, target, mult)
    return best


def _params(*sem):
    return pltpu.CompilerParams(dimension_semantics=sem, vmem_limit_bytes=VMEM_LIMIT)


def _rms(x, g):
    return x * lax.rsqrt(jnp.mean(x * x, axis=-1, keepdims=True) + RMS_EPS) * g


def _sigmoid(x):
    return 1.0 / (1.0 + jnp.exp(-x))


def _col_tile(n, fits):
    for tn in (1024, 512, 256, 128):
        if n % tn == 0 and fits(tn):
            return tn
    raise ValueError(f"no column tile of {n} fits VMEM")


def _dense_kernel(*refs, norm, n_w, act, has_bias, has_resid):
    it = iter(refs)
    a_ref = next(it)
    g_ref = next(it) if norm else None
    w_refs = [next(it) for _ in range(n_w)]
    b_ref = next(it) if has_bias else None
    r_ref = next(it) if has_resid else None
    o_ref = next(it)
    if norm:
        h_s = next(it)

        @pl.when(pl.program_id(1) == 0)
        def _():
            h_s[...] = _rms(a_ref[...], g_ref[...]).astype(BF16)

        a = h_s[...]
    else:
        a = a_ref[...]
    acc = jnp.dot(a, w_refs[0][...].astype(BF16), preferred_element_type=F32)
    if has_bias:
        acc = acc + b_ref[...]
    if act == "gelu":
        acc = 0.5 * acc * (1.0 + lax.erf(acc * (1.0 / math.sqrt(2.0))))
    elif act == "swiglu":
        up = jnp.dot(a, w_refs[1][...].astype(BF16), preferred_element_type=F32)
        acc = acc * _sigmoid(acc) * up
    if has_resid:
        acc = r_ref[...] + acc
    o_ref[...] = acc.astype(o_ref.dtype)


def dense_matmul(a, w, li, *, norm_g=None, w2=None, bias=None, act=None, resid=None, out_dtype=F32,
                 name="dense_matmul"):
    m, k = a.shape
    n = w.shape[-1]
    norm = norm_g is not None
    n_w = 2 if w2 is not None else 1
    ab = a.dtype.itemsize
    ob = jnp.dtype(out_dtype).itemsize

    def est(tm, tn, a_bufs):
        e = tm * k * ab * a_bufs + (tm * k * 2 if norm else 0)
        e += n_w * k * tn * (4 * 2 + 2)
        e += n_w * tm * tn * 4
        e += tm * tn * ob * 2 + (tm * tn * 4 * 2 if resid is not None else 0)
        return e

    tm, tn, a_bufs = None, None, None
    for cand in sorted({t for t in range(256, min(m, MAX_ROW_TILE) + 1, 256) if m % t == 0} | {_pick(m, ROW_TILE, 16)},
                       reverse=True):
        for bufs in (2, 1):
            fit = [t for t in (512, 256) if n % t == 0 and est(cand, t, bufs) <= VMEM_BUDGET]
            if fit:
                tm, tn, a_bufs = cand, fit[0], bufs
                break
        if tm is not None:
            break
    assert tm is not None, (m, k, n)
    a_mode = {} if a_bufs == 2 else {"pipeline_mode": pl.Buffered(1)}
    in_specs = [pl.BlockSpec((tm, k), lambda i, j: (i, 0), **a_mode)]
    args = [a]
    if norm:
        in_specs.append(pl.BlockSpec((1, k), lambda i, j: (0, 0)))
        args.append(norm_g.reshape(1, k))
    for ww in (w, w2)[:n_w]:
        in_specs.append(pl.BlockSpec((None, k, tn), lambda i, j: (li, 0, j)))
        args.append(ww)
    if bias is not None:
        in_specs.append(pl.BlockSpec((None, 1, tn), lambda i, j: (li, 0, j)))
        args.append(bias.reshape(bias.shape[0], 1, n))
    if resid is not None:
        in_specs.append(pl.BlockSpec((tm, tn), lambda i, j: (i, j)))
        args.append(resid)
    return pl.pallas_call(
        functools.partial(_dense_kernel, norm=norm, n_w=n_w, act=act, has_bias=bias is not None,
                          has_resid=resid is not None),
        grid=(m // tm, n // tn),
        in_specs=in_specs,
        out_specs=pl.BlockSpec((tm, tn), lambda i, j: (i, j)),
        out_shape=jax.ShapeDtypeStruct((m, n), out_dtype),
        scratch_shapes=[pltpu.VMEM((tm, k), BF16)] if norm else [],
        compiler_params=_params("parallel", "arbitrary"),
        name=name,
    )(*args)


def _grouped_kernel(be_ref, bv_ref, bx_ref, a_ref, *refs, n_w):
    w_refs, o_ref = refs[:n_w], refs[n_w]
    nv = bv_ref[pl.program_id(0)]
    tm = a_ref.shape[0]

    def compute(rows):
        a = a_ref[rows, :]
        acc = jnp.dot(a, w_refs[0][...].astype(BF16), preferred_element_type=F32)
        if n_w == 2:
            up = jnp.dot(a, w_refs[1][...].astype(BF16), preferred_element_type=F32)
            acc = acc * _sigmoid(acc) * up
        o_ref[rows, :] = acc.astype(o_ref.dtype)

    @pl.when(nv == tm)
    def _():
        compute(slice(0, tm))

    @pl.when(jnp.logical_and(nv > 0, nv < tm))
    def _():
        for s in range(tm // SUB_ROWS):
            rows = slice(s * SUB_ROWS, (s + 1) * SUB_ROWS)

            @pl.when(s * SUB_ROWS < nv)
            def _():
                compute(rows)

            @pl.when(s * SUB_ROWS >= nv)
            def _():
                o_ref[rows, :] = jnp.zeros((SUB_ROWS, o_ref.shape[1]), o_ref.dtype)

    @pl.when(nv == 0)
    def _():
        o_ref[...] = jnp.zeros(o_ref.shape, o_ref.dtype)


def grouped_matmul(a, w, block_e, block_valid, block_x, *, w2=None, out_dtype=F32, name="grouped_matmul"):
    m, k = a.shape
    n = w.shape[-1]
    nb = block_e.shape[0]
    assert m == nb * ROW_TILE
    tm = ROW_TILE
    n_w = 2 if w2 is not None else 1
    ob = jnp.dtype(out_dtype).itemsize

    def est(tn):
        return (tm * k * 2 * 2 + n_w * k * tn * (4 * 2 + 2) + n_w * tm * tn * 4 + tm * tn * ob * 2)

    tn = _col_tile(n, lambda t: t <= 512 and est(t) <= VMEM_BUDGET)
    nj = n // tn

    def a_map(b, j, be, bv, bx):
        return (bx[b], 0)

    def w_map(b, j, be, bv, bx):
        return (be[b], 0, jnp.where(bv[b] > 0, j, nj - 1))

    in_specs = [pl.BlockSpec((tm, k), a_map)]
    in_specs += [pl.BlockSpec((None, k, tn), w_map)] * n_w
    return pl.pallas_call(
        functools.partial(_grouped_kernel, n_w=n_w),
        grid_spec=pltpu.PrefetchScalarGridSpec(
            num_scalar_prefetch=3,
            grid=(nb, nj),
            in_specs=in_specs,
            out_specs=pl.BlockSpec((tm, tn), lambda b, j, be, bv, bx: (b, j)),
        ),
        out_shape=jax.ShapeDtypeStruct((m, n), out_dtype),
        compiler_params=_params("arbitrary", "arbitrary"),
        name=name,
    )(block_e, block_valid, block_x, a, *((w, w2)[:n_w]))


def _mixer_mid_kernel(a_ref, g_ref, p_ref, ha_ref, hg_ref, hp_ref, cw_ref, cb_ref, lng_ref, lnb_ref,
                      pw_ref, ps_ref, mid_ref, clast_ref, cx_s, px_s, y_s, pool_s, sh_s,
                      *, halo_glu, zero_first, pos0):
    t = pl.program_id(1)
    tt, cw = a_ref.shape
    pw = p_ref.shape[1]
    ck = cw_ref.shape[0]
    off = HALO - (ck - 1)

    c = a_ref[...] * _sigmoid(g_ref[...])
    ch = ha_ref[...]
    if halo_glu:
        ch = ch * _sigmoid(hg_ref[...])
    ph = hp_ref[...]
    if zero_first:
        ch = jnp.where(t > 0, ch, 0.0)
        ph = jnp.where(t > 0, ph, 0.0)
    cx_s[0:HALO, :] = ch
    cx_s[HALO:HALO + tt, :] = c
    px_s[0:HALO, :] = ph
    px_s[HALO:HALO + tt, :] = p_ref[...]

    @pl.when(t == pl.num_programs(1) - 1)
    def _():
        clast_ref[0] = c[tt - HALO:tt, :]

    rc = min(tt, 32)
    lc = min(cw, 512)
    n_sh = HALO + tt - 8
    for s in range(1, 8):
        sh_s[s - 1, 0:n_sh, :] = cx_s[s:s + n_sh, :]
    for r0 in range(0, tt, rc):
        for l0 in range(0, cw, lc):
            acc = jnp.broadcast_to(cb_ref[:, l0:l0 + lc], (rc, lc))
            for k in range(ck):
                q, s = divmod(off + k, 8)
                base = r0 + 8 * q
                if s == 0:
                    tap = cx_s[base:base + rc, l0:l0 + lc]
                else:
                    tap = sh_s[s - 1, base:base + rc, l0:l0 + lc]
                acc = acc + cw_ref[k:k + 1, l0:l0 + lc] * tap
            y_s[r0:r0 + rc, l0:l0 + lc] = acc

    y = y_s[...]
    mu = jnp.mean(y, axis=-1, keepdims=True)
    yc = y - mu
    var = jnp.mean(yc * yc, axis=-1, keepdims=True)
    yn = yc * lax.rsqrt(var + LN_EPS) * lng_ref[...] + lnb_ref[...]
    mid_ref[:, 0:cw] = (yn * _sigmoid(yn)).astype(mid_ref.dtype)

    n_grp = len(POOL_WINDOWS)
    gw = pw // n_grp
    for gi, win in enumerate(POOL_WINDOWS):
        lanes = slice(gi * gw, (gi + 1) * gw)
        for r0 in range(0, tt, rc):
            cur = px_s[HALO + r0:HALO + r0 + rc, lanes]
            ws = cur
            for i in range(1, win):
                ws = ws + px_s[HALO + r0 - i:HALO + r0 - i + rc, lanes]
            pos = pos0 + t * tt + r0 + lax.broadcasted_iota(jnp.int32, (rc, 1), 0)
            cnt = jnp.minimum(pos + 1, win).astype(F32)
            pool_s[r0:r0 + rc, lanes] = ws / cnt - cur
        mixed = jnp.dot(pool_s[:, lanes].astype(BF16), pw_ref[gi].astype(BF16),
                        preferred_element_type=F32)
        mid_ref[:, cw + gi * gw:cw + (gi + 1) * gw] = (mixed * ps_ref[:, lanes]).astype(mid_ref.dtype)


def mixer_mid(z, row0, n_seq, seq_len, tt, halo_arrays, halo_map, halo_glu, zero_first, pos0,
              conv_w, conv_b, ln_g, ln_b, pool_w, pool_scale):
    ck, cw = conv_w.shape
    pw = pool_scale.shape[0]
    assert cw == pw and seq_len % tt == 0 and tt >= HALO and row0 % tt == 0 and ck - 1 <= HALO
    nt = seq_len // tt
    base = row0 // tt

    def cur_map(col):
        return lambda b, t: (base + b * nt + t, col)

    ha, hg, hp = halo_arrays
    in_specs = [pl.BlockSpec((tt, cw), cur_map(0)),
                pl.BlockSpec((tt, cw), cur_map(1)),
                pl.BlockSpec((tt, pw), cur_map(2)),
                pl.BlockSpec((HALO, cw), halo_map(0)),
                pl.BlockSpec((HALO, cw), halo_map(1)),
                pl.BlockSpec((HALO, pw), halo_map(2)),
                pl.BlockSpec((ck, cw), lambda b, t: (0, 0)),
                pl.BlockSpec((1, cw), lambda b, t: (0, 0)),
                pl.BlockSpec((1, cw), lambda b, t: (0, 0)),
                pl.BlockSpec((1, cw), lambda b, t: (0, 0)),
                pl.BlockSpec(pool_w.shape, lambda b, t: (0, 0, 0)),
                pl.BlockSpec((1, pw), lambda b, t: (0, 0))]
    return pl.pallas_call(
        functools.partial(_mixer_mid_kernel, halo_glu=halo_glu, zero_first=zero_first, pos0=pos0),
        grid=(n_seq, nt),
        in_specs=in_specs,
        out_specs=[pl.BlockSpec((tt, cw + pw), lambda b, t: (b * nt + t, 0)),
                   pl.BlockSpec((1, HALO, cw), lambda b, t: (b, 0, 0))],
        out_shape=[jax.ShapeDtypeStruct((n_seq * seq_len, cw + pw), BF16),
                   jax.ShapeDtypeStruct((n_seq, HALO, cw), F32)],
        scratch_shapes=[pltpu.VMEM((HALO + tt, cw), F32), pltpu.VMEM((HALO + tt, pw), F32),
                        pltpu.VMEM((tt, cw), F32), pltpu.VMEM((tt, pw), F32),
                        pltpu.VMEM((7, HALO + tt, cw), F32)],
        compiler_params=_params("parallel", "arbitrary"),
        name="mixer_mid",
    )(z, z, z, ha, hg, hp, conv_w, conv_b.reshape(1, cw), ln_g.reshape(1, cw), ln_b.reshape(1, cw),
      pool_w, pool_scale.reshape(1, pw))


def _spatial_kernel(u_ref, v_ref, lng_ref, lnb_ref, ws_ref, bs_ref, o_ref, vn_ref, *, n_prompt_blocks, sub):
    i = pl.program_id(0)
    blk = u_ref.shape[0]
    n_heads = ws_ref.shape[1]
    hd = u_ref.shape[1] // n_heads

    v = v_ref[...]
    mu = jnp.mean(v, axis=-1, keepdims=True)
    vc = v - mu
    var = jnp.mean(vc * vc, axis=-1, keepdims=True)
    vn = vc * lax.rsqrt(var + LN_EPS) * lng_ref[...] + lnb_ref[...]

    @pl.when(i >= n_prompt_blocks)
    def _():
        vn_ref[...] = vn

    vb = vn.astype(BF16)
    ri = lax.broadcasted_iota(jnp.int32, (blk, blk), 0)
    ci = lax.broadcasted_iota(jnp.int32, (blk, blk), 1)
    is_sample = i >= n_prompt_blocks
    rq = jnp.where(is_sample, ri // sub, ri // CHUNK)
    cq = jnp.where(is_sample, ci // sub, ci // CHUNK)
    mask = jnp.logical_and(cq <= rq, cq >= jnp.where(is_sample, rq, 0))
    for h in range(n_heads):
        w = jnp.where(mask, ws_ref[0, h], 0.0).astype(BF16)
        s = jnp.dot(w, vb[:, h * hd:(h + 1) * hd], preferred_element_type=F32) + bs_ref[0, :, h:h + 1]
        o_ref[:, h * hd:(h + 1) * hd] = (u_ref[:, h * hd:(h + 1) * hd] * s).astype(o_ref.dtype)


def spatial_gate(z, n_prompt_rows, sub, ln_g, ln_b, w_all, b_all):
    m, two_gw = z.shape
    gw = two_gw // 2
    blk = GMLP_BLOCK
    assert m % blk == 0 and n_prompt_rows % blk == 0
    npb = n_prompt_rows // blk
    nblk = m // blk
    return pl.pallas_call(
        functools.partial(_spatial_kernel, n_prompt_blocks=npb, sub=sub),
        grid=(nblk,),
        in_specs=[pl.BlockSpec((blk, gw), lambda i: (i, 0)),
                  pl.BlockSpec((blk, gw), lambda i: (i, 1)),
                  pl.BlockSpec((1, gw), lambda i: (0, 0)),
                  pl.BlockSpec((1, gw), lambda i: (0, 0)),
                  pl.BlockSpec((1,) + w_all.shape[1:], lambda i: (jnp.where(i >= npb, 1, 0), 0, 0, 0)),
                  pl.BlockSpec((1,) + b_all.shape[1:], lambda i: (jnp.where(i >= npb, 1, 0), 0, 0))],
        out_specs=[pl.BlockSpec((blk, gw), lambda i: (i, 0)),
                   pl.BlockSpec((blk, gw), lambda i: (jnp.maximum(i - npb, 0), 0))],
        out_shape=[jax.ShapeDtypeStruct((m, gw), BF16),
                   jax.ShapeDtypeStruct((m - n_prompt_rows, gw), F32)],
        compiler_params=_params("arbitrary"),
        name="spatial_gate",
    )(z, z, ln_g.reshape(1, gw), ln_b.reshape(1, gw), w_all, b_all)


def _router_kernel(x_ref, g_ref, wr_ref, xn_ref, idx_ref, gate_ref):
    xn = _rms(x_ref[...], g_ref[...])
    xn_ref[...] = xn
    n_e = wr_ref.shape[0]
    tb = x_ref.shape[0]
    logits = [jnp.sum(xn * wr_ref[e:e + 1, :], axis=-1, keepdims=True) for e in range(n_e)]
    m1 = logits[0]
    i1 = jnp.zeros((tb, 1), jnp.int32)
    for e in range(1, n_e):
        better = logits[e] > m1
        m1 = jnp.where(better, logits[e], m1)
        i1 = jnp.where(better, e, i1)
    m2 = jnp.full((tb, 1), -jnp.inf, F32)
    i2 = jnp.zeros((tb, 1), jnp.int32)
    for e in range(n_e):
        better = jnp.logical_and(i1 != e, logits[e] > m2)
        m2 = jnp.where(better, logits[e], m2)
        i2 = jnp.where(better, e, i2)
    ex = jnp.exp(m2 - m1)
    den = 1.0 + ex
    lane = lax.broadcasted_iota(jnp.int32, idx_ref.shape, 1)
    idx_ref[...] = jnp.where(lane == 0, i1, jnp.where(lane == 1, i2, 0))
    gate_ref[...] = jnp.where(lane == 0, 1.0 / den, jnp.where(lane == 1, ex / den, 0.0))


def router(x, norm_g, w_router):
    m, d = x.shape
    n_e = w_router.shape[1]
    tb = _pick(m, 256, 8)
    return pl.pallas_call(
        _router_kernel,
        grid=(m // tb,),
        in_specs=[pl.BlockSpec((tb, d), lambda i: (i, 0)),
                  pl.BlockSpec((1, d), lambda i: (0, 0)),
                  pl.BlockSpec((n_e, d), lambda i: (0, 0))],
        out_specs=[pl.BlockSpec((tb, d), lambda i: (i, 0)),
                   pl.BlockSpec((tb, 128), lambda i: (i, 0)),
                   pl.BlockSpec((tb, 128), lambda i: (i, 0))],
        out_shape=[jax.ShapeDtypeStruct((m, d), F32),
                   jax.ShapeDtypeStruct((m, 128), jnp.int32),
                   jax.ShapeDtypeStruct((m, 128), F32)],
        compiler_params=_params("parallel"),
        name="router",
    )(x, norm_g.reshape(1, d), w_router.T)


def _gather_kernel(idx_ref, bv_ref, x_hbm, o_ref, buf, sem):
    b = pl.program_id(0)
    tg = buf.shape[0]
    cnt = (bv_ref[b] + SUB_ROWS - 1) // SUB_ROWS * SUB_ROWS

    def issue(q, carry):
        for u in range(ISSUE_UNROLL):
            r = q * ISSUE_UNROLL + u
            tok = idx_ref[b * tg + r]
            pltpu.make_async_copy(x_hbm.at[pl.ds(tok, 1), :], buf.at[pl.ds(r, 1), :], sem).start()
        return carry

    lax.fori_loop(0, cnt // ISSUE_UNROLL, issue, 0)

    @pl.when(cnt > 0)
    def _():
        pltpu.make_async_copy(x_hbm.at[pl.ds(0, cnt), :], buf.at[pl.ds(0, cnt), :], sem).wait()

    for s in range(tg // SUB_ROWS):
        rows = slice(s * SUB_ROWS, (s + 1) * SUB_ROWS)

        @pl.when(s * SUB_ROWS < cnt)
        def _():
            o_ref[rows, :] = buf[rows, :].astype(o_ref.dtype)

        @pl.when(s * SUB_ROWS >= cnt)
        def _():
            o_ref[rows, :] = jnp.zeros((SUB_ROWS, o_ref.shape[1]), o_ref.dtype)


def gather_rows(x, idx, block_valid, out_dtype):
    n_out = idx.shape[0]
    d = x.shape[1]
    tg = ROW_TILE
    assert n_out % tg == 0 and x.shape[0] >= tg
    return pl.pallas_call(
        _gather_kernel,
        grid_spec=pltpu.PrefetchScalarGridSpec(
            num_scalar_prefetch=2,
            grid=(n_out // tg,),
            in_specs=[pl.BlockSpec(memory_space=pl.ANY)],
            out_specs=pl.BlockSpec((tg, d), lambda i, idx, bv: (i, 0)),
            scratch_shapes=[pltpu.VMEM((tg, d), F32), pltpu.SemaphoreType.DMA(())],
        ),
        out_shape=jax.ShapeDtypeStruct((n_out, d), out_dtype),
        compiler_params=_params("arbitrary"),
        name="gather_rows",
    )(idx, block_valid, x)


def _combine_kernel(pos_ref, rows_hbm, gate_ref, r_ref, *refs, n_split):
    if n_split is not None:
        g_ref, o_ref, o2_ref, buf, sem = refs
    else:
        o_ref, buf, sem = refs
    i = pl.program_id(0)
    tb = r_ref.shape[0]
    per_iter = ISSUE_UNROLL // TOP_K

    def issue(q, carry):
        for u in range(per_iter):
            r = q * per_iter + u
            for k in range(TOP_K):
                p = pos_ref[(i * tb + r) * TOP_K + k]
                pltpu.make_async_copy(rows_hbm.at[pl.ds(p, 1), :], buf.at[k, pl.ds(r, 1), :], sem).start()
        return carry

    lax.fori_loop(0, tb // per_iter, issue, 0)
    for k in range(TOP_K):
        pltpu.make_async_copy(rows_hbm.at[pl.ds(0, tb), :], buf.at[k], sem).wait()
    y = gate_ref[:, 0:1] * buf[0] + gate_ref[:, 1:2] * buf[1]
    x = r_ref[...] + y
    if n_split is None:
        o_ref[...] = x
    else:
        @pl.when(i < n_split)
        def _():
            o_ref[...] = _rms(x, g_ref[...])

        @pl.when(i >= n_split)
        def _():
            o2_ref[...] = _rms(x, g_ref[...])


def combine(rows, pos, gate, resid, final_g=None, split_rows=None):
    m, d = resid.shape
    final = final_g is not None
    tb = _pick(math.gcd(m, split_rows) if final else m, 256, 16)
    assert rows.shape[0] >= tb and tb % (ISSUE_UNROLL // TOP_K) == 0
    in_specs = [pl.BlockSpec(memory_space=pl.ANY),
                pl.BlockSpec((tb, 128), lambda i, pos: (i, 0)),
                pl.BlockSpec((tb, d), lambda i, pos: (i, 0))]
    args = [rows, gate, resid]
    if final:
        n_split = split_rows // tb
        in_specs.append(pl.BlockSpec((1, d), lambda i, pos: (0, 0)))
        args.append(final_g.reshape(1, d))
        out_specs = [pl.BlockSpec((tb, d), lambda i, pos: (jnp.minimum(i, n_split - 1), 0)),
                     pl.BlockSpec((tb, d), lambda i, pos: (jnp.maximum(i - n_split, 0), 0))]
        out_shape = [jax.ShapeDtypeStruct((split_rows, d), F32),
                     jax.ShapeDtypeStruct((m - split_rows, d), F32)]
    else:
        n_split = None
        out_specs = pl.BlockSpec((tb, d), lambda i, pos: (i, 0))
        out_shape = jax.ShapeDtypeStruct((m, d), F32)
    return pl.pallas_call(
        functools.partial(_combine_kernel, n_split=n_split),
        grid_spec=pltpu.PrefetchScalarGridSpec(
            num_scalar_prefetch=1,
            grid=(m // tb,),
            in_specs=in_specs,
            out_specs=out_specs,
            scratch_shapes=[pltpu.VMEM((TOP_K, tb, d), F32), pltpu.SemaphoreType.DMA(())],
        ),
        out_shape=out_shape,
        compiler_params=_params("arbitrary"),
        name="combine",
    )(pos.reshape(-1), *args)


def _moe_layout(top_e, n_experts, first_expert):
    tm = ROW_TILE
    n_tok = top_e.shape[0]
    n_asg = n_tok * TOP_K
    flat_e = top_e.reshape(-1)
    onehot = (flat_e[:, None] == jnp.arange(n_experts)[None, :]).astype(jnp.int32)
    rank = jnp.sum((jnp.cumsum(onehot, axis=0) - onehot) * onehot, axis=1)
    counts = jnp.sum(onehot, axis=0)
    padded = (counts + tm - 1) // tm * tm
    pends = jnp.cumsum(padded)
    pstarts = pends - padded
    dest = pstarts[flat_e] + rank
    n_blocks = -(-n_asg // tm) + n_experts
    n_slots = n_blocks * tm
    src_tok = (jnp.arange(n_slots, dtype=jnp.int32) % n_tok).at[dest].set(
        jnp.arange(n_asg, dtype=jnp.int32) // TOP_K)
    bstart = jnp.arange(n_blocks, dtype=jnp.int32) * tm
    be = jnp.minimum(jnp.sum((pends[None, :] <= bstart[:, None]).astype(jnp.int32), axis=1), n_experts - 1)
    bv = jnp.clip(counts[be] - (bstart - pstarts[be]), 0, tm).astype(jnp.int32)
    n_used = jnp.maximum(pends[-1] // tm, 1).astype(jnp.int32)
    last = n_used - 1
    live = jnp.arange(n_blocks) < n_used
    be = jnp.where(live, be, be[last]).astype(jnp.int32) + first_expert
    bx = jnp.where(live, jnp.arange(n_blocks, dtype=jnp.int32), last)
    bv = jnp.where(live, bv, 0)
    return dest.reshape(n_tok, TOP_K).astype(jnp.int32), src_tok, be, bv, bx


def _even_layer(x, li, mix_norm, np_rows, dims, state_conv, state_pool, w_in, conv_w, conv_b, ln_g, ln_b,
                pool_w, pool_scale, w_out, ffn_norm, w_gate, w_up, w_down):
    n_b, seq, d_b, d_seq = dims
    ck, cw = conv_w.shape
    pw = pool_scale.shape[0]
    z = dense_matmul(x, w_in, li, norm_g=mix_norm, name="even_in")
    tt = _pick(seq, 256, HALO)

    def prompt_halo(col):
        return lambda b, t: (jnp.maximum((b * seq + t * tt) // HALO - 1, 0), col)

    mid_p, cl_p = mixer_mid(z, 0, n_b, seq, tt, (z, z, z), prompt_halo, True, True, 0,
                            conv_w, conv_b, ln_g, ln_b, pool_w, pool_scale)
    sc = jnp.pad(state_conv, ((0, 0), (HALO - (ck - 1), 0), (0, 0))).reshape(d_b * HALO, cw)
    sp = jnp.pad(state_pool, ((0, 0), (HALO - state_pool.shape[1], 0), (0, 0))).reshape(d_b * HALO, pw)

    def sample_halo(col):
        return lambda b, t: (b, 0)

    mid_s, cl_s = mixer_mid(z, np_rows, d_b, d_seq, d_seq, (sc, sc, sp), sample_halo, False, False,
                            PAST_LEN, conv_w, conv_b, ln_g, ln_b, pool_w, pool_scale)
    mid = jnp.concatenate([mid_p, mid_s], axis=0)
    x = dense_matmul(mid, w_out, li, resid=x, name="even_out")
    hmid = dense_matmul(x, w_gate, li, norm_g=ffn_norm, w2=w_up, act="swiglu", out_dtype=BF16,
                        name="ffn_gate_up")
    x = dense_matmul(hmid, w_down, li, resid=x, name="ffn_down")
    n_hist = state_pool.shape[1]
    xp_p = z[:np_rows, 2 * cw:].reshape(n_b, seq, pw)[:, seq - n_hist:]
    xp_s = z[np_rows:, 2 * cw:].reshape(d_b, d_seq, pw)[:, d_seq - n_hist:]
    new_state = (cl_p[:, HALO - (ck - 1):], xp_p, cl_s[:, HALO - (ck - 1):], xp_s)
    return x, new_state


def _odd_layer(x, li, mix_norm, np_rows, dims, w_in, b_in, ln_g, ln_b, w_s, b_s, w_out, ffn_norm, w_router,
               w_gate, w_up, w_down, final_g):
    n_b, seq, d_b, d_seq = dims
    n_e = w_router.shape[1]
    z = dense_matmul(x, w_in, li, norm_g=mix_norm, bias=b_in, act="gelu", name="gmlp_in")
    gw = z.shape[1] // 2
    rep = GMLP_BLOCK // d_seq
    w_all = jnp.stack([w_s, jnp.tile(w_s[:, :d_seq, :d_seq], (1, rep, rep))])
    b_all = jnp.stack([jnp.transpose(b_s), jnp.transpose(jnp.tile(b_s[:, :d_seq], (1, rep)))])
    b_all = jnp.pad(b_all, ((0, 0), (0, 0), (0, 128 - b_all.shape[2])))
    gated, vn_s = spatial_gate(z, np_rows, d_seq, ln_g, ln_b, w_all, b_all)
    x = dense_matmul(gated, w_out, li, resid=x, name="gmlp_out")
    xn, idx, gate = router(x, ffn_norm, w_router)
    pos, src_tok, be, bv, bx = _moe_layout(idx[:, :TOP_K], n_e, li * n_e)
    xs = gather_rows(xn, src_tok, bv, BF16)
    experts = lambda w: w.reshape((-1,) + w.shape[2:])
    hmid = grouped_matmul(xs, experts(w_gate), be, bv, bx, w2=experts(w_up), out_dtype=BF16,
                          name="expert_gate_up")
    rows = grouped_matmul(hmid, experts(w_down), be, bv, bx, name="expert_down")
    x = combine(rows, pos, gate, x, final_g, np_rows)
    return x, vn_s.reshape(d_b, d_seq, gw)


def kernel(x_prompt, x_sample, state_conv, state_pool, even_mix_norm, even_w_in, even_conv_w, even_conv_b, even_conv_ln_g, even_conv_ln_b, even_pool_w, even_pool_scale, even_w_out, even_ffn_norm, even_ffn_w_gate, even_ffn_w_up, even_ffn_w_down, odd_mix_norm, odd_w_in, odd_b_in, odd_v_ln_g, odd_v_ln_b, odd_w_s, odd_b_s, odd_w_out, odd_ffn_norm, odd_router_w, odd_moe_w_gate, odd_moe_w_up, odd_moe_w_down, final_norm):
    n_b, seq, d = x_prompt.shape
    d_b, d_seq, _ = x_sample.shape
    assert GMLP_BLOCK % d_seq == 0 and d_seq <= CHUNK and seq % GMLP_BLOCK == 0 and d_seq >= HALO
    dims = (n_b, seq, d_b, d_seq)
    np_rows = n_b * seq
    depth = even_w_in.shape[0] + odd_w_in.shape[0]
    assert depth % 2 == 0, "the final rmsnorm is fused into the last expert combine"
    x = jnp.concatenate([x_prompt.reshape(np_rows, d), x_sample.reshape(d_b * d_seq, d)], axis=0)
    conv_p, pool_p, conv_s, pool_s, v_s = [], [], [], [], []
    for layer in range(depth):
        li = layer // 2
        if layer % 2 == 0:
            x, st = _even_layer(
                x, li, even_mix_norm[li], np_rows, dims, state_conv[li], state_pool[li], even_w_in,
                even_conv_w[li], even_conv_b[li], even_conv_ln_g[li], even_conv_ln_b[li], even_pool_w[li],
                even_pool_scale[li], even_w_out, even_ffn_norm[li], even_ffn_w_gate, even_ffn_w_up,
                even_ffn_w_down)
            conv_p.append(st[0])
            pool_p.append(st[1])
            conv_s.append(st[2])
            pool_s.append(st[3])
        else:
            x, vs = _odd_layer(
                x, li, odd_mix_norm[li], np_rows, dims, odd_w_in, odd_b_in, odd_v_ln_g[li], odd_v_ln_b[li],
                odd_w_s[li], odd_b_s[li], odd_w_out, odd_ffn_norm[li], odd_router_w[li],
                odd_moe_w_gate, odd_moe_w_up, odd_moe_w_down, final_norm if layer + 1 == depth else None)
            v_s.append(vs)
    y_prompt, y_sample = x
    return (y_prompt.reshape(n_b, seq, d), y_sample.reshape(d_b, d_seq, d),
            jnp.stack(conv_p), jnp.stack(pool_p), jnp.stack(conv_s), jnp.stack(pool_s), jnp.stack(v_s))
```

```python
import functools
import math

import jax
import jax.numpy as jnp
from jax import lax
from jax.experimental import pallas as pl
from jax.experimental.pallas import tpu as pltpu

F32 = jnp.float32
BF16 = jnp.bfloat16

CHUNK = 64
GMLP_BLOCK = 128
POOL_WINDOWS = (2, 4, 8, 16)
PAST_LEN = 1024
TOP_K = 2
RMS_EPS = 1e-6
LN_EPS = 1e-5

HALO = 32
VMEM_LIMIT = 60000 * 1024
VMEM_BUDGET = 50 * 1024 * 1024
ROW_TILE = 1024
MAX_ROW_TILE = 1536
SUB_ROWS = 256
ISSUE_UNROLL = 8


def _pick(n, target, mult):
    best = None
    for t in range(mult, min(n, target) + 1, mult):
        if n % t == 0:
            best = t
    assert best is not None, (n, target, mult)
    return best


def _params(*sem):
    return pltpu.CompilerParams(dimension_semantics=sem, vmem_limit_bytes=VMEM_LIMIT)


def _rms(x, g):
    return x * lax.rsqrt(jnp.mean(x * x, axis=-1, keepdims=True) + RMS_EPS) * g


def _sigmoid(x):
    return 1.0 / (1.0 + jnp.exp(-x))


def _col_tile(n, fits):
    for tn in (1024, 512, 256, 128):
        if n % tn == 0 and fits(tn):
            return tn
    raise ValueError(f"no column tile of {n} fits VMEM")


def _dense_kernel(*refs, norm, n_w, act, has_bias, has_resid, split):
    it = iter(refs)
    a_ref = next(it)
    g_ref = next(it) if norm else None
    w_refs = [next(it) for _ in range(n_w)]
    b_ref = next(it) if has_bias else None
    r_ref = next(it) if has_resid else None
    o_ref = next(it)
    o2_ref = next(it) if split is not None else None
    if norm:
        h_s = next(it)

        @pl.when(pl.program_id(1) == 0)
        def _():
            h_s[...] = _rms(a_ref[...], g_ref[...]).astype(BF16)

        a = h_s[...]
    else:
        a = a_ref[...]
    acc = jnp.dot(a, w_refs[0][...].astype(BF16), preferred_element_type=F32)
    if has_bias:
        acc = acc + b_ref[...]
    if act == "gelu":
        acc = 0.5 * acc * (1.0 + lax.erf(acc * (1.0 / math.sqrt(2.0))))
    elif act == "swiglu":
        up = jnp.dot(a, w_refs[1][...].astype(BF16), preferred_element_type=F32)
        acc = acc * _sigmoid(acc) * up
    if has_resid:
        acc = r_ref[...] + acc
    if split is None:
        o_ref[...] = acc.astype(o_ref.dtype)
    else:
        @pl.when(pl.program_id(1) < split)
        def _():
            o_ref[...] = acc.astype(o_ref.dtype)

        @pl.when(pl.program_id(1) >= split)
        def _():
            o2_ref[...] = acc.astype(o2_ref.dtype)


def dense_matmul(a, w, li, *, norm_g=None, w2=None, bias=None, act=None, resid=None, out_dtype=F32,
                 halves_dtype=None, max_rows=None, name="dense_matmul"):
    m, k = a.shape
    n = w.shape[-1]
    norm = norm_g is not None
    n_w = 2 if w2 is not None else 1
    ab = a.dtype.itemsize
    ob = max(jnp.dtype(d).itemsize for d in (halves_dtype or (out_dtype,)))
    max_rows = max_rows or MAX_ROW_TILE

    def est(tm, tn, a_bufs):
        e = tm * k * ab * a_bufs + (tm * k * 2 if norm else 0)
        e += n_w * k * tn * (4 * 2 + 2)
        e += n_w * tm * tn * 4
        e += tm * tn * ob * 2 + (tm * tn * 4 * 2 if resid is not None else 0)
        return e

    tm, tn, a_bufs = None, None, None
    n_half = n // 2 if halves_dtype else n
    for cand in sorted({t for t in range(256, min(m, max_rows) + 1, 256) if m % t == 0} | {_pick(m, ROW_TILE, 16)},
                       reverse=True):
        for bufs in (2, 1):
            fit = [t for t in (512, 256) if n_half % t == 0 and est(cand, t, bufs) <= VMEM_BUDGET]
            if fit:
                tm, tn, a_bufs = cand, fit[0], bufs
                break
        if tm is not None:
            break
    assert tm is not None, (m, k, n)
    a_mode = {} if a_bufs == 2 else {"pipeline_mode": pl.Buffered(1)}
    in_specs = [pl.BlockSpec((tm, k), lambda i, j: (i, 0), **a_mode)]
    args = [a]
    if norm:
        in_specs.append(pl.BlockSpec((1, k), lambda i, j: (0, 0)))
        args.append(norm_g.reshape(1, k))
    for ww in (w, w2)[:n_w]:
        in_specs.append(pl.BlockSpec((None, k, tn), lambda i, j: (li, 0, j)))
        args.append(ww)
    if bias is not None:
        in_specs.append(pl.BlockSpec((None, 1, tn), lambda i, j: (li, 0, j)))
        args.append(bias.reshape(bias.shape[0], 1, n))
    if resid is not None:
        in_specs.append(pl.BlockSpec((tm, tn), lambda i, j: (i, j)))
        args.append(resid)
    if halves_dtype:
        split = n_half // tn
        out_specs = [pl.BlockSpec((tm, tn), lambda i, j: (i, jnp.minimum(j, split - 1))),
                     pl.BlockSpec((tm, tn), lambda i, j: (i, jnp.maximum(j - split, 0)))]
        out_shape = [jax.ShapeDtypeStruct((m, n_half), d) for d in halves_dtype]
    else:
        split = None
        out_specs = pl.BlockSpec((tm, tn), lambda i, j: (i, j))
        out_shape = jax.ShapeDtypeStruct((m, n), out_dtype)
    return pl.pallas_call(
        functools.partial(_dense_kernel, norm=norm, n_w=n_w, act=act, has_bias=bias is not None,
                          has_resid=resid is not None, split=split),
        grid=(m // tm, n // tn),
        in_specs=in_specs,
        out_specs=out_specs,
        out_shape=out_shape,
        scratch_shapes=[pltpu.VMEM((tm, k), BF16)] if norm else [],
        compiler_params=_params("parallel", "arbitrary"),
        name=name,
    )(*args)


def _grouped_kernel(be_ref, bv_ref, bx_ref, a_ref, *refs, n_w):
    w_refs, o_ref = refs[:n_w], refs[n_w]
    nv = bv_ref[pl.program_id(0)]
    tm = a_ref.shape[0]

    def compute(rows):
        a = a_ref[rows, :]
        acc = jnp.dot(a, w_refs[0][...].astype(BF16), preferred_element_type=F32)
        if n_w == 2:
            up = jnp.dot(a, w_refs[1][...].astype(BF16), preferred_element_type=F32)
            acc = acc * _sigmoid(acc) * up
        o_ref[rows, :] = acc.astype(o_ref.dtype)

    @pl.when(nv == tm)
    def _():
        compute(slice(0, tm))

    @pl.when(jnp.logical_and(nv > 0, nv < tm))
    def _():
        for s in range(tm // SUB_ROWS):
            rows = slice(s * SUB_ROWS, (s + 1) * SUB_ROWS)

            @pl.when(s * SUB_ROWS < nv)
            def _():
                compute(rows)

            @pl.when(s * SUB_ROWS >= nv)
            def _():
                o_ref[rows, :] = jnp.zeros((SUB_ROWS, o_ref.shape[1]), o_ref.dtype)

    @pl.when(nv == 0)
    def _():
        o_ref[...] = jnp.zeros(o_ref.shape, o_ref.dtype)


def grouped_matmul(a, w, block_e, block_valid, block_x, *, w2=None, out_dtype=F32, name="grouped_matmul"):
    m, k = a.shape
    n = w.shape[-1]
    nb = block_e.shape[0]
    assert m == nb * ROW_TILE
    tm = ROW_TILE
    n_w = 2 if w2 is not None else 1
    ob = jnp.dtype(out_dtype).itemsize

    def est(tn):
        return (tm * k * 2 * 2 + n_w * k * tn * (4 * 2 + 2) + n_w * tm * tn * 4 + tm * tn * ob * 2)

    tn = _col_tile(n, lambda t: t <= 512 and est(t) <= VMEM_BUDGET)
    nj = n // tn

    def a_map(b, j, be, bv, bx):
        return (bx[b], 0)

    def w_map(b, j, be, bv, bx):
        return (be[b], 0, jnp.where(bv[b] > 0, j, nj - 1))

    in_specs = [pl.BlockSpec((tm, k), a_map)]
    in_specs += [pl.BlockSpec((None, k, tn), w_map)] * n_w
    return pl.pallas_call(
        functools.partial(_grouped_kernel, n_w=n_w),
        grid_spec=pltpu.PrefetchScalarGridSpec(
            num_scalar_prefetch=3,
            grid=(nb, nj),
            in_specs=in_specs,
            out_specs=pl.BlockSpec((tm, tn), lambda b, j, be, bv, bx: (b, j)),
        ),
        out_shape=jax.ShapeDtypeStruct((m, n), out_dtype),
        compiler_params=_params("arbitrary", "arbitrary"),
        name=name,
    )(block_e, block_valid, block_x, a, *((w, w2)[:n_w]))


def _mixer_mid_kernel(a_ref, g_ref, p_ref, ha_ref, hg_ref, hp_ref, cw_ref, cb_ref, lng_ref, lnb_ref,
                      pw_ref, ps_ref, mid_ref, clast_ref, cx_s, px_s, y_s, pool_s, sh_s,
                      *, halo_glu, zero_first, pos0):
    t = pl.program_id(1)
    tt, cw = a_ref.shape
    pw = p_ref.shape[1]
    ck = cw_ref.shape[0]
    off = HALO - (ck - 1)

    c = a_ref[...] * _sigmoid(g_ref[...])
    ch = ha_ref[...]
    if halo_glu:
        ch = ch * _sigmoid(hg_ref[...])
    ph = hp_ref[...]
    if zero_first:
        ch = jnp.where(t > 0, ch, 0.0)
        ph = jnp.where(t > 0, ph, 0.0)
    cx_s[0:HALO, :] = ch
    cx_s[HALO:HALO + tt, :] = c
    px_s[0:HALO, :] = ph
    px_s[HALO:HALO + tt, :] = p_ref[...]

    @pl.when(t == pl.num_programs(1) - 1)
    def _():
        clast_ref[0] = c[tt - HALO:tt, :]

    rc = min(tt, 32)
    lc = min(cw, 512)
    n_sh = HALO + tt - 8
    for s in range(1, 8):
        sh_s[s - 1, 0:n_sh, :] = cx_s[s:s + n_sh, :]
    for r0 in range(0, tt, rc):
        for l0 in range(0, cw, lc):
            acc = jnp.broadcast_to(cb_ref[:, l0:l0 + lc], (rc, lc))
            for k in range(ck):
                q, s = divmod(off + k, 8)
                base = r0 + 8 * q
                if s == 0:
                    tap = cx_s[base:base + rc, l0:l0 + lc]
                else:
                    tap = sh_s[s - 1, base:base + rc, l0:l0 + lc]
                acc = acc + cw_ref[k:k + 1, l0:l0 + lc] * tap
            y_s[r0:r0 + rc, l0:l0 + lc] = acc

    y = y_s[...]
    mu = jnp.mean(y, axis=-1, keepdims=True)
    yc = y - mu
    var = jnp.mean(yc * yc, axis=-1, keepdims=True)
    yn = yc * lax.rsqrt(var + LN_EPS) * lng_ref[...] + lnb_ref[...]
    mid_ref[:, 0:cw] = (yn * _sigmoid(yn)).astype(mid_ref.dtype)

    n_grp = len(POOL_WINDOWS)
    gw = pw // n_grp
    for gi, win in enumerate(POOL_WINDOWS):
        lanes = slice(gi * gw, (gi + 1) * gw)
        for r0 in range(0, tt, rc):
            cur = px_s[HALO + r0:HALO + r0 + rc, lanes]
            ws = cur
            for i in range(1, win):
                ws = ws + px_s[HALO + r0 - i:HALO + r0 - i + rc, lanes]
            pos = pos0 + t * tt + r0 + lax.broadcasted_iota(jnp.int32, (rc, 1), 0)
            cnt = jnp.minimum(pos + 1, win).astype(F32)
            pool_s[r0:r0 + rc, lanes] = ws / cnt - cur
        mixed = jnp.dot(pool_s[:, lanes].astype(BF16), pw_ref[gi].astype(BF16),
                        preferred_element_type=F32)
        mid_ref[:, cw + gi * gw:cw + (gi + 1) * gw] = (mixed * ps_ref[:, lanes]).astype(mid_ref.dtype)


def mixer_mid(z, row0, n_seq, seq_len, tt, halo_arrays, halo_map, halo_glu, zero_first, pos0,
              conv_w, conv_b, ln_g, ln_b, pool_w, pool_scale):
    ck, cw = conv_w.shape
    pw = pool_scale.shape[0]
    assert cw == pw and seq_len % tt == 0 and tt >= HALO and row0 % tt == 0 and ck - 1 <= HALO
    nt = seq_len // tt
    base = row0 // tt

    def cur_map(col):
        return lambda b, t: (base + b * nt + t, col)

    ha, hg, hp = halo_arrays
    in_specs = [pl.BlockSpec((tt, cw), cur_map(0)),
                pl.BlockSpec((tt, cw), cur_map(1)),
                pl.BlockSpec((tt, pw), cur_map(2)),
                pl.BlockSpec((HALO, cw), halo_map(0)),
                pl.BlockSpec((HALO, cw), halo_map(1)),
                pl.BlockSpec((HALO, pw), halo_map(2)),
                pl.BlockSpec((ck, cw), lambda b, t: (0, 0)),
                pl.BlockSpec((1, cw), lambda b, t: (0, 0)),
                pl.BlockSpec((1, cw), lambda b, t: (0, 0)),
                pl.BlockSpec((1, cw), lambda b, t: (0, 0)),
                pl.BlockSpec(pool_w.shape, lambda b, t: (0, 0, 0)),
                pl.BlockSpec((1, pw), lambda b, t: (0, 0))]
    return pl.pallas_call(
        functools.partial(_mixer_mid_kernel, halo_glu=halo_glu, zero_first=zero_first, pos0=pos0),
        grid=(n_seq, nt),
        in_specs=in_specs,
        out_specs=[pl.BlockSpec((tt, cw + pw), lambda b, t: (b * nt + t, 0)),
                   pl.BlockSpec((1, HALO, cw), lambda b, t: (b, 0, 0))],
        out_shape=[jax.ShapeDtypeStruct((n_seq * seq_len, cw + pw), BF16),
                   jax.ShapeDtypeStruct((n_seq, HALO, cw), F32)],
        scratch_shapes=[pltpu.VMEM((HALO + tt, cw), F32), pltpu.VMEM((HALO + tt, pw), F32),
                        pltpu.VMEM((tt, cw), F32), pltpu.VMEM((tt, pw), F32),
                        pltpu.VMEM((7, HALO + tt, cw), F32)],
        compiler_params=_params("parallel", "arbitrary"),
        name="mixer_mid",
    )(z, z, z, ha, hg, hp, conv_w, conv_b.reshape(1, cw), ln_g.reshape(1, cw), ln_b.reshape(1, cw),
      pool_w, pool_scale.reshape(1, pw))


def _spatial_kernel(u_ref, v_ref, lng_ref, lnb_ref, ws_ref, bs_ref, o_ref, vn_ref, *, n_prompt_blocks, sub):
    i = pl.program_id(0)
    blk = u_ref.shape[0]
    n_heads = ws_ref.shape[1]
    hd = u_ref.shape[1] // n_heads

    v = v_ref[...]
    mu = jnp.mean(v, axis=-1, keepdims=True)
    vc = v - mu
    var = jnp.mean(vc * vc, axis=-1, keepdims=True)
    vn = vc * lax.rsqrt(var + LN_EPS) * lng_ref[...] + lnb_ref[...]

    @pl.when(i >= n_prompt_blocks)
    def _():
        vn_ref[...] = vn

    vb = vn.astype(BF16)
    ri = lax.broadcasted_iota(jnp.int32, (blk, blk), 0)
    ci = lax.broadcasted_iota(jnp.int32, (blk, blk), 1)
    is_sample = i >= n_prompt_blocks
    rq = jnp.where(is_sample, ri // sub, ri // CHUNK)
    cq = jnp.where(is_sample, ci // sub, ci // CHUNK)
    mask = jnp.logical_and(cq <= rq, cq >= jnp.where(is_sample, rq, 0))
    for h in range(n_heads):
        w = jnp.where(mask, ws_ref[0, h], 0.0).astype(BF16)
        s = jnp.dot(w, vb[:, h * hd:(h + 1) * hd], preferred_element_type=F32) + bs_ref[0, :, h:h + 1]
        o_ref[:, h * hd:(h + 1) * hd] = (u_ref[:, h * hd:(h + 1) * hd] * s).astype(o_ref.dtype)


def spatial_gate(u, v, n_prompt_rows, sub, ln_g, ln_b, w_all, b_all):
    m, gw = u.shape
    blk = GMLP_BLOCK
    assert m % blk == 0 and n_prompt_rows % blk == 0
    npb = n_prompt_rows // blk
    nblk = m // blk
    return pl.pallas_call(
        functools.partial(_spatial_kernel, n_prompt_blocks=npb, sub=sub),
        grid=(nblk,),
        in_specs=[pl.BlockSpec((blk, gw), lambda i: (i, 0)),
                  pl.BlockSpec((blk, gw), lambda i: (i, 0)),
                  pl.BlockSpec((1, gw), lambda i: (0, 0)),
                  pl.BlockSpec((1, gw), lambda i: (0, 0)),
                  pl.BlockSpec((1,) + w_all.shape[1:], lambda i: (jnp.where(i >= npb, 1, 0), 0, 0, 0)),
                  pl.BlockSpec((1,) + b_all.shape[1:], lambda i: (jnp.where(i >= npb, 1, 0), 0, 0))],
        out_specs=[pl.BlockSpec((blk, gw), lambda i: (i, 0)),
                   pl.BlockSpec((blk, gw), lambda i: (jnp.maximum(i - npb, 0), 0))],
        out_shape=[jax.ShapeDtypeStruct((m, gw), BF16),
                   jax.ShapeDtypeStruct((m - n_prompt_rows, gw), F32)],
        compiler_params=_params("arbitrary"),
        name="spatial_gate",
    )(u, v, ln_g.reshape(1, gw), ln_b.reshape(1, gw), w_all, b_all)


def _router_kernel(x_ref, g_ref, wr_ref, xn_ref, idx_ref, gate_ref):
    xn = _rms(x_ref[...], g_ref[...])
    xn_ref[...] = xn
    n_e = wr_ref.shape[0]
    tb = x_ref.shape[0]
    logits = [jnp.sum(xn * wr_ref[e:e + 1, :], axis=-1, keepdims=True) for e in range(n_e)]
    m1 = logits[0]
    i1 = jnp.zeros((tb, 1), jnp.int32)
    for e in range(1, n_e):
        better = logits[e] > m1
        m1 = jnp.where(better, logits[e], m1)
        i1 = jnp.where(better, e, i1)
    m2 = jnp.full((tb, 1), -jnp.inf, F32)
    i2 = jnp.zeros((tb, 1), jnp.int32)
    for e in range(n_e):
        better = jnp.logical_and(i1 != e, logits[e] > m2)
        m2 = jnp.where(better, logits[e], m2)
        i2 = jnp.where(better, e, i2)
    ex = jnp.exp(m2 - m1)
    den = 1.0 + ex
    lane = lax.broadcasted_iota(jnp.int32, idx_ref.shape, 1)
    idx_ref[...] = jnp.where(lane == 0, i1, jnp.where(lane == 1, i2, 0))
    gate_ref[...] = jnp.where(lane == 0, 1.0 / den, jnp.where(lane == 1, ex / den, 0.0))


def router(x, norm_g, w_router):
    m, d = x.shape
    n_e = w_router.shape[1]
    tb = _pick(m, 256, 8)
    return pl.pallas_call(
        _router_kernel,
        grid=(m // tb,),
        in_specs=[pl.BlockSpec((tb, d), lambda i: (i, 0)),
                  pl.BlockSpec((1, d), lambda i: (0, 0)),
                  pl.BlockSpec((n_e, d), lambda i: (0, 0))],
        out_specs=[pl.BlockSpec((tb, d), lambda i: (i, 0)),
                   pl.BlockSpec((tb, 128), lambda i: (i, 0)),
                   pl.BlockSpec((tb, 128), lambda i: (i, 0))],
        out_shape=[jax.ShapeDtypeStruct((m, d), F32),
                   jax.ShapeDtypeStruct((m, 128), jnp.int32),
                   jax.ShapeDtypeStruct((m, 128), F32)],
        compiler_params=_params("parallel"),
        name="router",
    )(x, norm_g.reshape(1, d), w_router.T)


def _gather_kernel(idx_ref, bv_ref, x_hbm, o_ref, buf, sem):
    b = pl.program_id(0)
    tg = buf.shape[1]
    slot = b % 2

    def count(blk):
        return (bv_ref[blk] + SUB_ROWS - 1) // SUB_ROWS * SUB_ROWS

    def start(blk, to):
        def issue(q, carry):
            for u in range(ISSUE_UNROLL):
                r = q * ISSUE_UNROLL + u
                tok = idx_ref[blk * tg + r]
                pltpu.make_async_copy(x_hbm.at[pl.ds(tok, 1), :], buf.at[to, pl.ds(r, 1), :],
                                      sem.at[to]).start()
            return carry

        lax.fori_loop(0, count(blk) // ISSUE_UNROLL, issue, 0)

    @pl.when(b == 0)
    def _():
        start(0, 0)

    @pl.when(b + 1 < pl.num_programs(0))
    def _():
        start(b + 1, 1 - slot)

    cnt = count(b)

    @pl.when(cnt > 0)
    def _():
        pltpu.make_async_copy(x_hbm.at[pl.ds(0, cnt), :], buf.at[slot, pl.ds(0, cnt), :], sem.at[slot]).wait()

    for s in range(tg // SUB_ROWS):
        rows = slice(s * SUB_ROWS, (s + 1) * SUB_ROWS)

        @pl.when(s * SUB_ROWS < cnt)
        def _():
            o_ref[rows, :] = buf[slot, rows, :].astype(o_ref.dtype)

        @pl.when(s * SUB_ROWS >= cnt)
        def _():
            o_ref[rows, :] = jnp.zeros((SUB_ROWS, o_ref.shape[1]), o_ref.dtype)


def gather_rows(x, idx, block_valid, out_dtype):
    n_out = idx.shape[0]
    d = x.shape[1]
    tg = ROW_TILE
    assert n_out % tg == 0 and x.shape[0] >= tg
    return pl.pallas_call(
        _gather_kernel,
        grid_spec=pltpu.PrefetchScalarGridSpec(
            num_scalar_prefetch=2,
            grid=(n_out // tg,),
            in_specs=[pl.BlockSpec(memory_space=pl.ANY)],
            out_specs=pl.BlockSpec((tg, d), lambda i, idx, bv: (i, 0)),
            scratch_shapes=[pltpu.VMEM((2, tg, d), F32), pltpu.SemaphoreType.DMA((2,))],
        ),
        out_shape=jax.ShapeDtypeStruct((n_out, d), out_dtype),
        compiler_params=_params("arbitrary"),
        name="gather_rows",
    )(idx, block_valid, x)


def _combine_kernel(pos_ref, rows_hbm, gate_ref, r_ref, *refs, n_split):
    if n_split is not None:
        g_ref, o_ref, o2_ref, buf, sem = refs
    else:
        o_ref, buf, sem = refs
    i = pl.program_id(0)
    tb = r_ref.shape[0]
    per_iter = ISSUE_UNROLL // TOP_K
    slot = i % 2

    def start(blk, to):
        def issue(q, carry):
            for u in range(per_iter):
                r = q * per_iter + u
                for k in range(TOP_K):
                    p = pos_ref[(blk * tb + r) * TOP_K + k]
                    pltpu.make_async_copy(rows_hbm.at[pl.ds(p, 1), :], buf.at[to, k, pl.ds(r, 1), :],
                                          sem.at[to]).start()
            return carry

        lax.fori_loop(0, tb // per_iter, issue, 0)

    @pl.when(i == 0)
    def _():
        start(0, 0)

    @pl.when(i + 1 < pl.num_programs(0))
    def _():
        start(i + 1, 1 - slot)

    for k in range(TOP_K):
        pltpu.make_async_copy(rows_hbm.at[pl.ds(0, tb), :], buf.at[slot, k], sem.at[slot]).wait()
    y = gate_ref[:, 0:1] * buf[slot, 0] + gate_ref[:, 1:2] * buf[slot, 1]
    x = r_ref[...] + y
    if n_split is None:
        o_ref[...] = x
    else:
        @pl.when(i < n_split)
        def _():
            o_ref[...] = _rms(x, g_ref[...])

        @pl.when(i >= n_split)
        def _():
            o2_ref[...] = _rms(x, g_ref[...])


def combine(rows, pos, gate, resid, final_g=None, split_rows=None):
    m, d = resid.shape
    final = final_g is not None
    tb = _pick(math.gcd(m, split_rows) if final else m, 256, 16)
    assert rows.shape[0] >= tb and tb % (ISSUE_UNROLL // TOP_K) == 0
    in_specs = [pl.BlockSpec(memory_space=pl.ANY),
                pl.BlockSpec((tb, 128), lambda i, pos: (i, 0)),
                pl.BlockSpec((tb, d), lambda i, pos: (i, 0))]
    args = [rows, gate, resid]
    if final:
        n_split = split_rows // tb
        in_specs.append(pl.BlockSpec((1, d), lambda i, pos: (0, 0)))
        args.append(final_g.reshape(1, d))
        out_specs = [pl.BlockSpec((tb, d), lambda i, pos: (jnp.minimum(i, n_split - 1), 0)),
                     pl.BlockSpec((tb, d), lambda i, pos: (jnp.maximum(i - n_split, 0), 0))]
        out_shape = [jax.ShapeDtypeStruct((split_rows, d), F32),
                     jax.ShapeDtypeStruct((m - split_rows, d), F32)]
    else:
        n_split = None
        out_specs = pl.BlockSpec((tb, d), lambda i, pos: (i, 0))
        out_shape = jax.ShapeDtypeStruct((m, d), F32)
    return pl.pallas_call(
        functools.partial(_combine_kernel, n_split=n_split),
        grid_spec=pltpu.PrefetchScalarGridSpec(
            num_scalar_prefetch=1,
            grid=(m // tb,),
            in_specs=in_specs,
            out_specs=out_specs,
            scratch_shapes=[pltpu.VMEM((2, TOP_K, tb, d), F32), pltpu.SemaphoreType.DMA((2,))],
        ),
        out_shape=out_shape,
        compiler_params=_params("arbitrary"),
        name="combine",
    )(pos.reshape(-1), *args)


def _moe_layout(top_e, n_experts, first_expert):
    tm = ROW_TILE
    n_tok = top_e.shape[0]
    n_asg = n_tok * TOP_K
    flat_e = top_e.reshape(-1)
    onehot = (flat_e[:, None] == jnp.arange(n_experts)[None, :]).astype(jnp.int32)
    rank = jnp.sum((jnp.cumsum(onehot, axis=0) - onehot) * onehot, axis=1)
    counts = jnp.sum(onehot, axis=0)
    padded = (counts + tm - 1) // tm * tm
    pends = jnp.cumsum(padded)
    pstarts = pends - padded
    dest = pstarts[flat_e] + rank
    n_blocks = -(-n_asg // tm) + n_experts
    n_slots = n_blocks * tm
    src_tok = (jnp.arange(n_slots, dtype=jnp.int32) % n_tok).at[dest].set(
        jnp.arange(n_asg, dtype=jnp.int32) // TOP_K)
    bstart = jnp.arange(n_blocks, dtype=jnp.int32) * tm
    be = jnp.minimum(jnp.sum((pends[None, :] <= bstart[:, None]).astype(jnp.int32), axis=1), n_experts - 1)
    bv = jnp.clip(counts[be] - (bstart - pstarts[be]), 0, tm).astype(jnp.int32)
    n_used = jnp.maximum(pends[-1] // tm, 1).astype(jnp.int32)
    last = n_used - 1
    live = jnp.arange(n_blocks) < n_used
    be = jnp.where(live, be, be[last]).astype(jnp.int32) + first_expert
    bx = jnp.where(live, jnp.arange(n_blocks, dtype=jnp.int32), last)
    bv = jnp.where(live, bv, 0)
    return dest.reshape(n_tok, TOP_K).astype(jnp.int32), src_tok, be, bv, bx


def _even_layer(x, li, mix_norm, np_rows, dims, state_conv, state_pool, w_in, conv_w, conv_b, ln_g, ln_b,
                pool_w, pool_scale, w_out, ffn_norm, w_gate, w_up, w_down):
    n_b, seq, d_b, d_seq = dims
    ck, cw = conv_w.shape
    pw = pool_scale.shape[0]
    z = dense_matmul(x, w_in, li, norm_g=mix_norm, name="even_in")
    tt = _pick(seq, 256, HALO)

    def prompt_halo(col):
        return lambda b, t: (jnp.maximum((b * seq + t * tt) // HALO - 1, 0), col)

    mid_p, cl_p = mixer_mid(z, 0, n_b, seq, tt, (z, z, z), prompt_halo, True, True, 0,
                            conv_w, conv_b, ln_g, ln_b, pool_w, pool_scale)
    sc = jnp.pad(state_conv, ((0, 0), (HALO - (ck - 1), 0), (0, 0))).reshape(d_b * HALO, cw)
    sp = jnp.pad(state_pool, ((0, 0), (HALO - state_pool.shape[1], 0), (0, 0))).reshape(d_b * HALO, pw)

    def sample_halo(col):
        return lambda b, t: (b, 0)

    mid_s, cl_s = mixer_mid(z, np_rows, d_b, d_seq, d_seq, (sc, sc, sp), sample_halo, False, False,
                            PAST_LEN, conv_w, conv_b, ln_g, ln_b, pool_w, pool_scale)
    mid = jnp.concatenate([mid_p, mid_s], axis=0)
    x = dense_matmul(mid, w_out, li, resid=x, name="even_out")
    hmid = dense_matmul(x, w_gate, li, norm_g=ffn_norm, w2=w_up, act="swiglu", out_dtype=BF16,
                        max_rows=ROW_TILE, name="ffn_gate_up")
    x = dense_matmul(hmid, w_down, li, resid=x, max_rows=ROW_TILE, name="ffn_down")
    n_hist = state_pool.shape[1]
    xp_p = z[:np_rows, 2 * cw:].reshape(n_b, seq, pw)[:, seq - n_hist:]
    xp_s = z[np_rows:, 2 * cw:].reshape(d_b, d_seq, pw)[:, d_seq - n_hist:]
    new_state = (cl_p[:, HALO - (ck - 1):], xp_p, cl_s[:, HALO - (ck - 1):], xp_s)
    return x, new_state


def _odd_layer(x, li, mix_norm, np_rows, dims, w_in, b_in, ln_g, ln_b, w_s, b_s, w_out, ffn_norm, w_router,
               w_gate, w_up, w_down, final_g):
    n_b, seq, d_b, d_seq = dims
    n_e = w_router.shape[1]
    u, v = dense_matmul(x, w_in, li, norm_g=mix_norm, bias=b_in, act="gelu", halves_dtype=(BF16, F32),
                        name="gmlp_in")
    gw = u.shape[1]
    rep = GMLP_BLOCK // d_seq
    w_all = jnp.stack([w_s, jnp.tile(w_s[:, :d_seq, :d_seq], (1, rep, rep))])
    b_all = jnp.stack([jnp.transpose(b_s), jnp.transpose(jnp.tile(b_s[:, :d_seq], (1, rep)))])
    b_all = jnp.pad(b_all, ((0, 0), (0, 0), (0, 128 - b_all.shape[2])))
    gated, vn_s = spatial_gate(u, v, np_rows, d_seq, ln_g, ln_b, w_all, b_all)
    x = dense_matmul(gated, w_out, li, resid=x, name="gmlp_out")
    xn, idx, gate = router(x, ffn_norm, w_router)
    pos, src_tok, be, bv, bx = _moe_layout(idx[:, :TOP_K], n_e, li * n_e)
    xs = gather_rows(xn, src_tok, bv, BF16)
    experts = lambda w: w.reshape((-1,) + w.shape[2:])
    hmid = grouped_matmul(xs, experts(w_gate), be, bv, bx, w2=experts(w_up), out_dtype=BF16,
                          name="expert_gate_up")
    rows = grouped_matmul(hmid, experts(w_down), be, bv, bx, name="expert_down")
    x = combine(rows, pos, gate, x, final_g, np_rows)
    return x, vn_s.reshape(d_b, d_seq, gw)


def kernel(x_prompt, x_sample, state_conv, state_pool, even_mix_norm, even_w_in, even_conv_w, even_conv_b, even_conv_ln_g, even_conv_ln_b, even_pool_w, even_pool_scale, even_w_out, even_ffn_norm, even_ffn_w_gate, even_ffn_w_up, even_ffn_w_down, odd_mix_norm, odd_w_in, odd_b_in, odd_v_ln_g, odd_v_ln_b, odd_w_s, odd_b_s, odd_w_out, odd_ffn_norm, odd_router_w, odd_moe_w_gate, odd_moe_w_up, odd_moe_w_down, final_norm):
    n_b, seq, d = x_prompt.shape
    d_b, d_seq, _ = x_sample.shape
    assert GMLP_BLOCK % d_seq == 0 and d_seq <= CHUNK and seq % GMLP_BLOCK == 0 and d_seq >= HALO
    dims = (n_b, seq, d_b, d_seq)
    np_rows = n_b * seq
    depth = even_w_in.shape[0] + odd_w_in.shape[0]
    assert depth % 2 == 0, "the final rmsnorm is fused into the last expert combine"
    x = jnp.concatenate([x_prompt.reshape(np_rows, d), x_sample.reshape(d_b * d_seq, d)], axis=0)
    conv_p, pool_p, conv_s, pool_s, v_s = [], [], [], [], []
    for layer in range(depth):
        li = layer // 2
        if layer % 2 == 0:
            x, st = _even_layer(
                x, li, even_mix_norm[li], np_rows, dims, state_conv[li], state_pool[li], even_w_in,
                even_conv_w[li], even_conv_b[li], even_conv_ln_g[li], even_conv_ln_b[li], even_pool_w[li],
                even_pool_scale[li], even_w_out, even_ffn_norm[li], even_ffn_w_gate, even_ffn_w_up,
                even_ffn_w_down)
            conv_p.append(st[0])
            pool_p.append(st[1])
            conv_s.append(st[2])
            pool_s.append(st[3])
        else:
            x, vs = _odd_layer(
                x, li, odd_mix_norm[li], np_rows, dims, odd_w_in, odd_b_in, odd_v_ln_g[li], odd_v_ln_b[li],
                odd_w_s[li], odd_b_s[li], odd_w_out, odd_ffn_norm[li], odd_router_w[li],
                odd_moe_w_gate, odd_moe_w_up, odd_moe_w_down, final_norm if layer + 1 == depth else None)
            v_s.append(vs)
    y_prompt, y_sample = x
    return (y_prompt.reshape(n_b, seq, d), y_sample.reshape(d_b, d_seq, d),
            jnp.stack(conv_p), jnp.stack(pool_p), jnp.stack(conv_s), jnp.stack(pool_s), jnp.stack(v_s))
```
